```python
import jax
import jax.numpy as jnp
from jax import lax
import numpy as np

D_MODEL = 2048
BATCH = 4
SEQ = 4096
DEPTH = 1
DEC_BATCH = 1
DEC_SEQ = 16384
PAST_LEN = 128

N_META = 16
HG_HEADS = 8
HG_DK = 128
HG_DV = 128
HG_KDIM = HG_HEADS * HG_DK
HG_WIDTH = HG_HEADS * HG_DV
HG_CHUNK = 64
HG_PAD = HG_CHUNK - N_META
N_Q_HEADS = 16
N_KV_HEADS = 4
HEAD_DIM = 64
GROUP = N_Q_HEADS // N_KV_HEADS
ATT_WIDTH = N_Q_HEADS * HEAD_DIM
KV_WIDTH = N_KV_HEADS * HEAD_DIM
WINDOW = 128
ATT_BLOCK = 128
D_FF = 4 * D_MODEL
EPS = 1e-6
IN_WIDTHS = (HG_KDIM, HG_WIDTH, HG_KDIM, HG_KDIM, HG_WIDTH, ATT_WIDTH, KV_WIDTH, KV_WIDTH, D_MODEL, D_MODEL)
IN_COLS = 3 * HG_KDIM + 2 * HG_WIDTH + ATT_WIDTH + 2 * KV_WIDTH + 2 * D_MODEL

kernel_name = "hybrid_hgrn2_swa_meta_encoder"


def rmsnorm(x, gain):
    x32 = x.astype(jnp.float32)
    y = x32 * lax.rsqrt(jnp.mean(x32 * x32, axis=-1, keepdims=True) + EPS)
    return (y * gain.astype(jnp.float32)).astype(x.dtype)


def split_columns(p):
    offsets = []
    acc = 0
    for w in IN_WIDTHS[:-1]:
        acc += w
        offsets.append(acc)
    return jnp.split(p, offsets, axis=-1)


def alibi_slopes():
    return jnp.exp2(-8.0 * jnp.arange(1, N_Q_HEADS + 1, dtype=jnp.float32) / N_Q_HEADS)


def gla_chunk_scan(q, k, v, logf):
    B, T, H, DK = q.shape
    DV = v.shape[-1]
    C = HG_CHUNK
    N = T // C
    q = q.reshape(B, N, C, H, DK)
    k = k.reshape(B, N, C, H, DK)
    v = v.reshape(B, N, C, H, DV)
    b = jnp.cumsum(logf.reshape(B, N, C, H, DK), axis=2)
    b_last = b[:, :, -1:]
    q_dec = q * jnp.exp(b)
    k_inv = k * jnp.exp(-b)
    k_end = k * jnp.exp(b_last - b)
    scores = jnp.einsum('bnthd,bnshd->bnhts', q_dec, k_inv)
    lower = jnp.tril(jnp.ones((C, C), dtype=bool))
    scores = jnp.where(lower, scores, 0.0)
    o_intra = jnp.einsum('bnhts,bnshv->bnthv', scores, v)
    inc = jnp.einsum('bnshd,bnshv->bnhdv', k_end, v)
    decay = jnp.exp(b_last[:, :, 0])

    def step(state, xs):
        dec, upd = xs
        return dec[..., None] * state + upd, state

    init = jnp.zeros((B, H, DK, DV), q.dtype)
    _, s_prev = lax.scan(step, init, (jnp.moveaxis(decay, 1, 0), jnp.moveaxis(inc, 1, 0)))
    o_inter = jnp.einsum('bnthd,nbhdv->bnthv', q_dec, s_prev)
    return (o_intra + o_inter).reshape(B, T, H, DV)


def hgrn2_mixer(q, i, f_fwd, f_bwd, g, lb_fwd, lb_bwd, out_gain):
    B, L, _ = q.shape
    dt = q.dtype
    f32 = jnp.float32
    qh = (jax.nn.silu(q.astype(f32)) * (HG_DK ** -0.5)).reshape(B, L, HG_HEADS, HG_DK)
    ih = i.astype(f32).reshape(B, L, HG_HEADS, HG_DV)

    def gates(f, lb):
        fg = lb + (1.0 - lb) * jax.nn.sigmoid(f.astype(f32))
        return (1.0 - fg).reshape(B, L, HG_HEADS, HG_DK), jnp.log(fg).reshape(B, L, HG_HEADS, HG_DK)

    def pad_front(a):
        return jnp.pad(a, ((0, 0), (HG_PAD, 0), (0, 0), (0, 0)))

    def flip(a):
        return jnp.flip(a, axis=1)

    k_f, logf_f = gates(f_fwd, lb_fwd)
    k_b, logf_b = gates(f_bwd, lb_bwd)
    qp, ip = pad_front(qh), pad_front(ih)
    o_f = gla_chunk_scan(qp, pad_front(k_f), ip, pad_front(logf_f))
    o_b = flip(gla_chunk_scan(flip(qp), flip(pad_front(k_b)), flip(ip), flip(pad_front(logf_b))))
    o = (o_f + o_b)[:, HG_PAD:]
    o = o * lax.rsqrt(jnp.mean(o * o, axis=-1, keepdims=True) + EPS)
    o = o.reshape(B, L, HG_WIDTH) * out_gain.astype(f32) * jax.nn.silu(g.astype(f32))
    return o.astype(dt)


def sink_softmax(scores, sink):
    m = jnp.maximum(jnp.max(scores, axis=-1, keepdims=True), sink)
    p = jnp.exp(scores - m)
    return p / (jnp.sum(p, axis=-1, keepdims=True) + jnp.exp(sink - m))


def window_attention(q, k, v, sink):
    B, L, _ = q.shape
    S = L - N_META
    nb = S // ATT_BLOCK
    dt = q.dtype
    f32 = jnp.float32
    q = (q.astype(f32) * (HEAD_DIM ** -0.5)).reshape(B, L, N_KV_HEADS, GROUP, HEAD_DIM)
    k = k.astype(f32).reshape(B, L, N_KV_HEADS, HEAD_DIM)
    v = v.astype(f32).reshape(B, L, N_KV_HEADS, HEAD_DIM)
    slopes = alibi_slopes().reshape(N_KV_HEADS, GROUP)
    sink = sink.astype(f32).reshape(N_KV_HEADS, GROUP)
    q_meta, q_tok = q[:, :N_META], q[:, N_META:]
    k_meta, k_tok = k[:, :N_META], k[:, N_META:]
    v_meta, v_tok = v[:, :N_META], v[:, N_META:]

    qb = q_tok.reshape(B, nb, ATT_BLOCK, N_KV_HEADS, GROUP, HEAD_DIM)

    def neighbours(a):
        ab = jnp.pad(a, ((0, 0), (ATT_BLOCK, ATT_BLOCK), (0, 0), (0, 0)))
        ab = ab.reshape(B, nb + 2, ATT_BLOCK, N_KV_HEADS, HEAD_DIM)
        return jnp.concatenate([ab[:, :-2], ab[:, 1:-1], ab[:, 2:]], axis=2)

    kw, vw = neighbours(k_tok), neighbours(v_tok)
    s_win = jnp.einsum('bnihgd,bnjhd->bhgnij', qb, kw)
    s_mk = jnp.einsum('bnihgd,bmhd->bhgnim', qb, k_meta)
    r = jnp.arange(ATT_BLOCK)[:, None]
    u = jnp.arange(3 * ATT_BLOCK)[None, :]
    dist = jnp.abs(u - ATT_BLOCK - r)
    key_pos = (jnp.arange(nb)[:, None, None] - 1) * ATT_BLOCK + u[None]
    valid = (dist <= WINDOW)[None] & (key_pos >= 0) & (key_pos < S)
    bias = -slopes[:, :, None, None, None] * dist.astype(f32)
    s_win = jnp.where(valid, s_win + bias, -jnp.inf)
    p = sink_softmax(jnp.concatenate([s_mk, s_win], axis=-1), sink[None, :, :, None, None, None])
    o_tok = (jnp.einsum('bhgnim,bmhd->bnihgd', p[..., :N_META], v_meta)
             + jnp.einsum('bhgnij,bnjhd->bnihgd', p[..., N_META:], vw))
    o_tok = o_tok.reshape(B, S, ATT_WIDTH)

    k_first, v_first = k_tok[:, :ATT_BLOCK], v_tok[:, :ATT_BLOCK]
    sm_m = jnp.einsum('bihgd,bmhd->bhgim', q_meta, k_meta)
    sm_t = jnp.einsum('bihgd,bjhd->bhgij', q_meta, k_first)
    mdist = N_META + jnp.arange(ATT_BLOCK)[None, :] - jnp.arange(N_META)[:, None]
    sm_t = jnp.where(mdist <= WINDOW, sm_t - slopes[:, :, None, None] * mdist.astype(f32), -jnp.inf)
    pm = sink_softmax(jnp.concatenate([sm_m, sm_t], axis=-1), sink[None, :, :, None, None])
    o_meta = (jnp.einsum('bhgim,bmhd->bihgd', pm[..., :N_META], v_meta)
              + jnp.einsum('bhgij,bjhd->bihgd', pm[..., N_META:], v_first))
    o_meta = o_meta.reshape(B, N_META, ATT_WIDTH)
    return jnp.concatenate([o_meta, o_tok], axis=1).astype(dt)


def encoder_layer(x, w_in, w_proj_hg, w_proj_att, w_out, w_ff1, w_ff2,
                  g_pre_mix, g_post_mix, g_pre_ff, g_post_ff, lb_fwd, lb_bwd, hg_out_gain, attn_sink):
    h = rmsnorm(x, g_pre_mix)
    (hq, hi, hf_f, hf_b, hg, aq, ak, av, gate_a, gate_b) = split_columns(h @ w_in)
    y_hg = hgrn2_mixer(hq, hi, hf_f, hf_b, hg, lb_fwd, lb_bwd, hg_out_gain) @ w_proj_hg
    y_att = window_attention(aq, ak, av, attn_sink) @ w_proj_att
    merged = jax.nn.sigmoid(gate_a) * y_hg + jax.nn.sigmoid(gate_b) * y_att
    x = x + rmsnorm(merged @ w_out, g_post_mix)
    h = rmsnorm(x, g_pre_ff)
    ff = jnp.square(jax.nn.relu(h @ w_ff1)) @ w_ff2
    return x + rmsnorm(ff, g_post_ff)


def encoder_trunk(x, meta_tokens, w_in, w_proj_hg, w_proj_att, w_out, w_ff1, w_ff2,
                  g_pre_mix, g_post_mix, g_pre_ff, g_post_ff, lb_logits, hg_out_gain, attn_sink):
    B = x.shape[0]
    meta = jnp.broadcast_to(meta_tokens[None].astype(x.dtype), (B, N_META, D_MODEL))
    x = jnp.concatenate([meta, x], axis=1)
    lbs = jnp.cumsum(jax.nn.softmax(lb_logits.astype(jnp.float32), axis=0), axis=0)
    for l in range(DEPTH):
        x = encoder_layer(x, w_in[l], w_proj_hg[l], w_proj_att[l], w_out[l], w_ff1[l], w_ff2[l],
                          g_pre_mix[l], g_post_mix[l], g_pre_ff[l], g_post_ff[l],
                          lbs[l, 0], lbs[l, 1], hg_out_gain[l], attn_sink[l])
    return x[:, N_META:]


def setup_inputs(seed: int = 0) -> dict:
    key = jax.random.key(seed)
    ks = jax.random.split(key, 16)
    f32 = jnp.float32

    def nrm(k, shape, scale):
        return jax.random.normal(k, shape, f32) * scale

    return {
        'x_prompt': nrm(ks[0], (BATCH, SEQ, D_MODEL), 1.0),
        'x_sample': nrm(ks[1], (DEC_BATCH, DEC_SEQ, D_MODEL), 1.0),
        'meta_tokens': nrm(ks[2], (N_META, D_MODEL), 1.0),
        'w_in': nrm(ks[3], (DEPTH, D_MODEL, IN_COLS), D_MODEL ** -0.5),
        'w_proj_hg': nrm(ks[4], (DEPTH, HG_WIDTH, D_MODEL), HG_WIDTH ** -0.5),
        'w_proj_att': nrm(ks[5], (DEPTH, ATT_WIDTH, D_MODEL), ATT_WIDTH ** -0.5),
        'w_out': nrm(ks[6], (DEPTH, D_MODEL, D_MODEL), D_MODEL ** -0.5),
        'w_ff1': nrm(ks[7], (DEPTH, D_MODEL, D_FF), D_MODEL ** -0.5),
        'w_ff2': nrm(ks[8], (DEPTH, D_FF, D_MODEL), D_FF ** -0.5),
        'g_pre_mix': 1.0 + nrm(ks[9], (DEPTH, D_MODEL), 0.01),
        'g_post_mix': 1.0 + nrm(ks[10], (DEPTH, D_MODEL), 0.01),
        'g_pre_ff': 1.0 + nrm(ks[11], (DEPTH, D_MODEL), 0.01),
        'g_post_ff': 1.0 + nrm(ks[12], (DEPTH, D_MODEL), 0.01),
        'lb_logits': nrm(ks[13], (DEPTH + 1, 2, HG_KDIM), 0.01),
        'hg_out_gain': 1.0 + nrm(ks[14], (DEPTH, HG_WIDTH), 0.01),
        'attn_sink': nrm(ks[15], (DEPTH, N_Q_HEADS), 0.5),
    }


def reference(x_prompt, x_sample, meta_tokens, w_in, w_proj_hg, w_proj_att, w_out, w_ff1, w_ff2,
              g_pre_mix, g_post_mix, g_pre_ff, g_post_ff, lb_logits, hg_out_gain, attn_sink):
    y_prompt = encoder_trunk(x_prompt, meta_tokens, w_in, w_proj_hg, w_proj_att, w_out, w_ff1, w_ff2,
                             g_pre_mix, g_post_mix, g_pre_ff, g_post_ff, lb_logits, hg_out_gain, attn_sink)
    y_sample = encoder_trunk(x_sample, meta_tokens, w_in, w_proj_hg, w_proj_att, w_out, w_ff1, w_ff2,
                             g_pre_mix, g_post_mix, g_pre_ff, g_post_ff, lb_logits, hg_out_gain, attn_sink)
    return (y_prompt, y_sample)
```

```python
import functools
import math

import jax
import jax.numpy as jnp
from jax import lax
from jax.experimental import pallas as pl
from jax.experimental.pallas import tpu as pltpu

F32 = jnp.float32
BF16 = jnp.bfloat16

D_MODEL = 2048
N_META = 16
HG_HEADS = 8
HG_DK = 128
HG_DV = 128
HG_KDIM = HG_HEADS * HG_DK
HG_WIDTH = HG_HEADS * HG_DV
HG_CHUNK = 64
N_Q_HEADS = 16
N_KV_HEADS = 4
HEAD_DIM = 64
GROUP = N_Q_HEADS // N_KV_HEADS
ATT_WIDTH = N_Q_HEADS * HEAD_DIM
KV_WIDTH = N_KV_HEADS * HEAD_DIM
WINDOW = 128
ATT_BLOCK = 128
D_FF = 4 * D_MODEL
EPS = 1e-6
IN_COLS = 3 * HG_KDIM + 2 * HG_WIDTH + ATT_WIDTH + 2 * KV_WIDTH + 2 * D_MODEL

VMEM_LIMIT_BYTES = 56 * 1024 * 1024

IN_TILE = 512
T_Q_END = HG_KDIM // IN_TILE
T_I_END = T_Q_END + HG_WIDTH // IN_TILE
T_F_END = T_I_END + 2 * HG_KDIM // IN_TILE
T_G_END = T_F_END + HG_WIDTH // IN_TILE
T_AQ_END = T_G_END + ATT_WIDTH // IN_TILE
T_KV_END = T_AQ_END + 2 * KV_WIDTH // IN_TILE
T_GATE_END = T_KV_END + 2 * D_MODEL // IN_TILE
P1_COLS = HG_KDIM + HG_WIDTH
LF_COLS = 2 * HG_KDIM
P2_COLS = HG_WIDTH + ATT_WIDTH + 2 * KV_WIDTH
PG_COLS = 2 * D_MODEL


def _sigmoid(x):
    return 1.0 / (1.0 + jnp.exp(-x))


def _rms_scale(x):
    return lax.rsqrt(jnp.mean(x * x, axis=-1, keepdims=True) + EPS)


def _in_proj_kernel(x_ref, g_ref, w_ref, lbl_ref, p1_ref, lf_ref, p2_ref, pg_ref, h_ref):
    j = pl.program_id(1)

    @pl.when(j == 0)
    def _():
        x = x_ref[...]
        h_ref[...] = (x * _rms_scale(x) * g_ref[...]).astype(BF16)

    acc = jnp.dot(h_ref[...], w_ref[...], preferred_element_type=F32)

    @pl.when(j < T_Q_END)
    def _():
        p1_ref[...] = (acc * _sigmoid(acc) * (HG_DK ** -0.5)).astype(BF16)

    @pl.when((j >= T_Q_END) & (j < T_I_END))
    def _():
        p1_ref[...] = acc.astype(BF16)

    @pl.when((j >= T_I_END) & (j < T_F_END))
    def _():
        logits = lbl_ref[...]
        e = jnp.exp(logits - jnp.max(logits, axis=0, keepdims=True))
        lb = e[0:1] / jnp.sum(e, axis=0, keepdims=True)
        lf_ref[...] = jnp.log(lb + (1.0 - lb) * _sigmoid(acc))

    @pl.when((j >= T_F_END) & (j < T_G_END))
    def _():
        p2_ref[...] = (acc * _sigmoid(acc)).astype(BF16)

    @pl.when((j >= T_G_END) & (j < T_AQ_END))
    def _():
        p2_ref[...] = (acc * (HEAD_DIM ** -0.5)).astype(BF16)

    @pl.when((j >= T_AQ_END) & (j < T_KV_END))
    def _():
        p2_ref[...] = acc.astype(BF16)

    @pl.when(j >= T_KV_END)
    def _():
        pg_ref[...] = _sigmoid(acc).astype(BF16)


def _in_proj(x, gain, w_bf16, lb_logits2, tm):
    rows = x.shape[0]
    assert rows % tm == 0

    def clamp(j, lo, hi):
        return jnp.minimum(jnp.maximum(j - lo, 0), hi - lo - 1)

    return pl.pallas_call(
        _in_proj_kernel,
        grid=(rows // tm, T_GATE_END),
        in_specs=[
            pl.BlockSpec((tm, D_MODEL), lambda i, j: (i, 0)),
            pl.BlockSpec((1, D_MODEL), lambda i, j: (0, 0)),
            pl.BlockSpec((D_MODEL, IN_TILE), lambda i, j: (0, j)),
            pl.BlockSpec((2, IN_TILE), lambda i, j: (0, clamp(j, T_I_END, T_F_END))),
        ],
        out_specs=[
            pl.BlockSpec((tm, IN_TILE), lambda i, j: (i, clamp(j, 0, T_I_END))),
            pl.BlockSpec((tm, IN_TILE), lambda i, j: (i, clamp(j, T_I_END, T_F_END))),
            pl.BlockSpec((tm, IN_TILE), lambda i, j: (i, clamp(j, T_F_END, T_KV_END))),
            pl.BlockSpec((tm, IN_TILE), lambda i, j: (i, clamp(j, T_KV_END, T_GATE_END))),
        ],
        out_shape=[
            jax.ShapeDtypeStruct((rows, P1_COLS), BF16),
            jax.ShapeDtypeStruct((rows, LF_COLS), F32),
            jax.ShapeDtypeStruct((rows, P2_COLS), BF16),
            jax.ShapeDtypeStruct((rows, PG_COLS), BF16),
        ],
        scratch_shapes=[pltpu.VMEM((tm, D_MODEL), BF16)],
        compiler_params=pltpu.CompilerParams(
            dimension_semantics=("arbitrary", "arbitrary"), vmem_limit_bytes=VMEM_LIMIT_BYTES),
        name="in_proj",
    )(x, gain, w_bf16, lb_logits2)


def _tri(n, lower):
    r = lax.broadcasted_iota(jnp.int32, (n, n), 0)
    c = lax.broadcasted_iota(jnp.int32, (n, n), 1)
    return (r >= c) if lower else (r <= c)


def _cumsum_rows(tri_bf16, lf):
    hi = lf.astype(BF16)
    lo = (lf - hi.astype(F32)).astype(BF16)
    r = jnp.dot(tri_bf16, jnp.concatenate([hi, lo], axis=1), preferred_element_type=F32)
    return r[:, :HG_DK] + r[:, HG_DK:]


def _gla_chunk(q, v, lf, s_ref, h, tri_bf16, mask, end_row):
    cum = _cumsum_rows(tri_bf16, lf)
    b_end = cum[end_row:end_row + 1, :]
    k = 1.0 - jnp.exp(lf)
    q_dec = (q.astype(F32) * jnp.exp(cum)).astype(BF16)
    k_inv = (k * jnp.exp(-cum)).astype(BF16)
    k_end = (k * jnp.exp(b_end - cum)).astype(BF16)
    scores = lax.dot_general(q_dec, k_inv, (((1,), (1,)), ((), ())), preferred_element_type=F32)
    scores = jnp.where(mask, scores, 0.0).astype(BF16)
    st = s_ref[h]
    o = jnp.dot(scores, v, preferred_element_type=F32)
    o = o + lax.dot_general(q_dec, st.astype(BF16), (((1,), (1,)), ((), ())), preferred_element_type=F32)
    inc = lax.dot_general(v, k_end, (((0,), (0,)), ((), ())), preferred_element_type=F32)
    s_ref[h] = st * jnp.exp(b_end) + inc
    return o


def _scan_kernel(qf_ref, vf_ref, lff_ref, qb_ref, vb_ref, lfb_ref, vm_ref, lfm_ref,
                 of_ref, ob_ref, sf_ref, sb_ref, *, nchunks):
    n = pl.program_id(1)
    c_len = HG_CHUNK
    lower = _tri(c_len, True)
    upper = _tri(c_len, False)
    tri_l = jnp.where(lower, 1.0, 0.0).astype(BF16)
    tri_u = jnp.where(upper, 1.0, 0.0).astype(BF16)

    @pl.when(n == 0)
    def _():
        sb_ref[...] = jnp.zeros_like(sb_ref)
        tri_m = jnp.where(_tri(N_META, True), 1.0, 0.0).astype(BF16)
        for h in range(HG_HEADS):
            sl = slice(h * HG_DK, (h + 1) * HG_DK)
            lf = lfm_ref[:, sl]
            cum = _cumsum_rows(tri_m, lf)
            k_end = ((1.0 - jnp.exp(lf)) * jnp.exp(cum[N_META - 1:N_META, :] - cum)).astype(BF16)
            sf_ref[h] = lax.dot_general(vm_ref[:, sl], k_end, (((0,), (0,)), ((), ())),
                                        preferred_element_type=F32)

    def body(c, carry):
        rf = pl.multiple_of(c * c_len, c_len)
        rb = pl.multiple_of((nchunks - 1 - c) * c_len, c_len)
        for h in range(HG_HEADS):
            sl = slice(h * HG_DK, (h + 1) * HG_DK)
            rows_f = pl.ds(rf, c_len)
            of_ref[rows_f, sl] = _gla_chunk(qf_ref[rows_f, sl], vf_ref[rows_f, sl], lff_ref[rows_f, sl],
                                            sf_ref, h, tri_l, lower, c_len - 1)
            rows_b = pl.ds(rb, c_len)
            ob_ref[rows_b, sl] = _gla_chunk(qb_ref[rows_b, sl], vb_ref[rows_b, sl], lfb_ref[rows_b, sl],
                                            sb_ref, h, tri_u, upper, 0)
        return carry

    lax.fori_loop(0, nchunks, body, 0)


def _scan(p1, lf, p1_meta, lf_meta, batch, seq, rb):
    rows = batch * seq
    assert seq % rb == 0 and rb % HG_CHUNK == 0
    nb = seq // rb

    def fwd(col):
        return lambda b, n: (b * nb + n, col)

    def bwd(col):
        return lambda b, n: (b * nb + nb - 1 - n, col)

    return pl.pallas_call(
        functools.partial(_scan_kernel, nchunks=rb // HG_CHUNK),
        grid=(batch, nb),
        in_specs=[
            pl.BlockSpec((rb, HG_KDIM), fwd(0)),
            pl.BlockSpec((rb, HG_WIDTH), fwd(1)),
            pl.BlockSpec((rb, HG_KDIM), fwd(0)),
            pl.BlockSpec((rb, HG_KDIM), bwd(0)),
            pl.BlockSpec((rb, HG_WIDTH), bwd(1)),
            pl.BlockSpec((rb, HG_KDIM), bwd(1)),
            pl.BlockSpec((N_META, HG_WIDTH), lambda b, n: (0, 1)),
            pl.BlockSpec((N_META, HG_KDIM), lambda b, n: (0, 0)),
        ],
        out_specs=[
            pl.BlockSpec((rb, HG_WIDTH), fwd(0)),
            pl.BlockSpec((rb, HG_WIDTH), bwd(0)),
        ],
        out_shape=[
            jax.ShapeDtypeStruct((rows, HG_WIDTH), F32),
            jax.ShapeDtypeStruct((rows, HG_WIDTH), F32),
        ],
        scratch_shapes=[
            pltpu.VMEM((HG_HEADS, HG_DV, HG_DK), F32),
            pltpu.VMEM((HG_HEADS, HG_DV, HG_DK), F32),
        ],
        compiler_params=pltpu.CompilerParams(
            dimension_semantics=("arbitrary", "arbitrary"), vmem_limit_bytes=VMEM_LIMIT_BYTES),
        name="hgrn2_scan",
    )(p1, p1, lf, p1, p1, lf, p1_meta, lf_meta)


N_KEYS = 3 * ATT_BLOCK + N_META


def _alibi_slope(head):
    return 2.0 ** (-8.0 * (head + 1) / N_Q_HEADS)


def _attn_kernel(sink_ref, q_ref, kvc_ref, kvp_ref, kvn_ref, kvm_ref, o_ref, *, seq, qb):
    n = pl.program_id(1)
    nsub = qb // ATT_BLOCK
    r = lax.broadcasted_iota(jnp.int32, (ATT_BLOCK, N_KEYS), 0)
    u = lax.broadcasted_iota(jnp.int32, (ATT_BLOCK, N_KEYS), 1)
    is_meta = u >= 3 * ATT_BLOCK
    dist = jnp.abs(u - ATT_BLOCK - r)
    dist_f = jnp.where(is_meta, 0, dist).astype(F32)
    for i in range(nsub):
        rows = slice(i * ATT_BLOCK, (i + 1) * ATT_BLOCK)
        kv_prev = kvp_ref[...] if i == 0 else kvc_ref[(i - 1) * ATT_BLOCK:i * ATT_BLOCK, :]
        kv_next = kvn_ref[...] if i == nsub - 1 else kvc_ref[(i + 1) * ATT_BLOCK:(i + 2) * ATT_BLOCK, :]
        kv_all = jnp.concatenate([kv_prev, kvc_ref[rows, :], kv_next, kvm_ref[...]], axis=0)
        key_pos = n * qb + (i - 1) * ATT_BLOCK + u
        valid = is_meta | ((dist <= WINDOW) & (key_pos >= 0) & (key_pos < seq))
        for h in range(N_KV_HEADS):
            k = kv_all[:, h * HEAD_DIM:(h + 1) * HEAD_DIM]
            v = kv_all[:, KV_WIDTH + h * HEAD_DIM:KV_WIDTH + (h + 1) * HEAD_DIM]
            for g in range(GROUP):
                head = h * GROUP + g
                cols = slice(head * HEAD_DIM, (head + 1) * HEAD_DIM)
                s = lax.dot_general(q_ref[rows, cols], k, (((1,), (1,)), ((), ())),
                                    preferred_element_type=F32)
                s = jnp.where(valid, s - _alibi_slope(head) * dist_f, -jnp.inf)
                sink = sink_ref[head]
                m = jnp.maximum(jnp.max(s, axis=-1, keepdims=True), sink)
                p = jnp.exp(s - m)
                denom = jnp.sum(p, axis=-1, keepdims=True) + jnp.exp(sink - m)
                o = jnp.dot(p.astype(BF16), v, preferred_element_type=F32) / denom
                o_ref[rows, cols] = o.astype(BF16)


def _attn(p2, p2_meta, sink, batch, seq, qb):
    rows = batch * seq
    assert seq % qb == 0 and qb % ATT_BLOCK == 0
    nb = seq // qb
    per = qb // ATT_BLOCK
    last = rows // ATT_BLOCK - 1
    q_col = HG_WIDTH // ATT_WIDTH
    kv_col = (HG_WIDTH + ATT_WIDTH) // (2 * KV_WIDTH)

    return pl.pallas_call(
        functools.partial(_attn_kernel, seq=seq, qb=qb),
        grid=(batch, nb),
        in_specs=[
            pl.BlockSpec(memory_space=pltpu.SMEM),
            pl.BlockSpec((qb, ATT_WIDTH), lambda b, n: (b * nb + n, q_col)),
            pl.BlockSpec((qb, 2 * KV_WIDTH), lambda b, n: (b * nb + n, kv_col)),
            pl.BlockSpec((ATT_BLOCK, 2 * KV_WIDTH),
                         lambda b, n: (jnp.maximum((b * nb + n) * per - 1, 0), kv_col)),
            pl.BlockSpec((ATT_BLOCK, 2 * KV_WIDTH),
                         lambda b, n: (jnp.minimum((b * nb + n + 1) * per, last), kv_col)),
            pl.BlockSpec((N_META, 2 * KV_WIDTH), lambda b, n: (0, kv_col)),
        ],
        out_specs=pl.BlockSpec((qb, ATT_WIDTH), lambda b, n: (b * nb + n, 0)),
        out_shape=jax.ShapeDtypeStruct((rows, ATT_WIDTH), BF16),
        compiler_params=pltpu.CompilerParams(
            dimension_semantics=("arbitrary", "arbitrary"), vmem_limit_bytes=VMEM_LIMIT_BYTES),
        name="window_attn",
    )(sink, p2, p2, p2, p2, p2_meta)


def _merge_kernel(of_ref, ob_ref, g_ref, att_ref, ga_ref, gb_ref, x_ref, wa_ref, wb_ref, wo_ref,
                  hgain_ref, gpost_ref, gpre_ref, x1_ref, h2_ref):
    o = of_ref[...] + ob_ref[...]
    parts = []
    for h in range(HG_HEADS):
        oh = o[:, h * HG_DV:(h + 1) * HG_DV]
        parts.append(oh * _rms_scale(oh))
    a = jnp.concatenate(parts, axis=1) * hgain_ref[...] * g_ref[...].astype(F32)
    y = ga_ref[...].astype(F32) * jnp.dot(a.astype(BF16), wa_ref[...], preferred_element_type=F32)
    y = y + gb_ref[...].astype(F32) * jnp.dot(att_ref[...], wb_ref[...], preferred_element_type=F32)
    z = jnp.dot(y.astype(BF16), wo_ref[...], preferred_element_type=F32)
    x1 = x_ref[...] + z * _rms_scale(z) * gpost_ref[...]
    x1_ref[...] = x1
    h2_ref[...] = (x1 * _rms_scale(x1) * gpre_ref[...]).astype(BF16)


def _merge(o_f, o_b, p2, att, pg, x, wa, wb, wo, hgain, gpost, gpre, tm):
    rows = x.shape[0]
    assert rows % tm == 0
    row = lambda c: (lambda i: (i, c))
    const = lambda i: (0, 0)
    single = pl.Buffered(1)
    return pl.pallas_call(
        _merge_kernel,
        grid=(rows // tm,),
        in_specs=[
            pl.BlockSpec((tm, HG_WIDTH), row(0)),
            pl.BlockSpec((tm, HG_WIDTH), row(0)),
            pl.BlockSpec((tm, HG_WIDTH), row(0)),
            pl.BlockSpec((tm, ATT_WIDTH), row(0)),
            pl.BlockSpec((tm, D_MODEL), row(0)),
            pl.BlockSpec((tm, D_MODEL), row(1)),
            pl.BlockSpec((tm, D_MODEL), row(0)),
            pl.BlockSpec((HG_WIDTH, D_MODEL), const, pipeline_mode=single),
            pl.BlockSpec((ATT_WIDTH, D_MODEL), const, pipeline_mode=single),
            pl.BlockSpec((D_MODEL, D_MODEL), const, pipeline_mode=single),
            pl.BlockSpec((1, HG_WIDTH), const),
            pl.BlockSpec((1, D_MODEL), const),
            pl.BlockSpec((1, D_MODEL), const),
        ],
        out_specs=[
            pl.BlockSpec((tm, D_MODEL), row(0)),
            pl.BlockSpec((tm, D_MODEL), row(0)),
        ],
        out_shape=[
            jax.ShapeDtypeStruct((rows, D_MODEL), F32),
            jax.ShapeDtypeStruct((rows, D_MODEL), BF16),
        ],
        compiler_params=pltpu.CompilerParams(
            dimension_semantics=("arbitrary",), vmem_limit_bytes=VMEM_LIMIT_BYTES),
        name="merge_out",
    )(o_f, o_b, p2, att, pg, pg, x, wa, wb, wo, hgain, gpost, gpre)


def _ffn_kernel(h_ref, x_ref, w1_ref, w2_ref, g_ref, y_ref, acc_ref):
    j = pl.program_id(1)
    u = jnp.maximum(jnp.dot(h_ref[...], w1_ref[...], preferred_element_type=F32), 0.0)
    part = jnp.dot((u * u).astype(BF16), w2_ref[...], preferred_element_type=F32)

    @pl.when(j == 0)
    def _():
        acc_ref[...] = part

    @pl.when(j > 0)
    def _():
        acc_ref[...] += part

    @pl.when(j == pl.num_programs(1) - 1)
    def _():
        ff = acc_ref[...]
        y_ref[...] = x_ref[...] + ff * _rms_scale(ff) * g_ref[...]


def _ffn(h2, x1, w1, w2, gpost, tm, tf):
    rows = x1.shape[0]
    assert rows % tm == 0 and D_FF % tf == 0
    return pl.pallas_call(
        _ffn_kernel,
        grid=(rows // tm, D_FF // tf),
        in_specs=[
            pl.BlockSpec((tm, D_MODEL), lambda i, j: (i, 0)),
            pl.BlockSpec((tm, D_MODEL), lambda i, j: (i, 0)),
            pl.BlockSpec((D_MODEL, tf), lambda i, j: (0, j)),
            pl.BlockSpec((tf, D_MODEL), lambda i, j: (j, 0)),
            pl.BlockSpec((1, D_MODEL), lambda i, j: (0, 0)),
        ],
        out_specs=pl.BlockSpec((tm, D_MODEL), lambda i, j: (i, 0)),
        out_shape=jax.ShapeDtypeStruct((rows, D_MODEL), F32),
        scratch_shapes=[pltpu.VMEM((tm, D_MODEL), F32)],
        compiler_params=pltpu.CompilerParams(
            dimension_semantics=("arbitrary", "arbitrary"), vmem_limit_bytes=VMEM_LIMIT_BYTES),
        name="ffn",
    )(h2, x1, w1, w2, gpost)


def _tile(n, pref):
    return pref if n % pref == 0 else math.gcd(n, pref)


def _trunk(x, meta_proj, w, batch, seq):
    rows = batch * seq
    xf = x.reshape(rows, D_MODEL)
    p1, lf, p2, pg = _in_proj(xf, w["g_pre_mix"], w["w_in"], w["lb_logits"], _tile(rows, 1024))
    p1_m, lf_m, p2_m = meta_proj
    o_f, o_b = _scan(p1, lf, p1_m, lf_m, batch, seq, _tile(seq, 512))
    att = _attn(p2, p2_m, w["attn_sink"], batch, seq, _tile(seq, 512))
    x1, h2 = _merge(o_f, o_b, p2, att, pg, xf, w["w_proj_hg"], w["w_proj_att"], w["w_out"],
                    w["hg_out_gain"], w["g_post_mix"], w["g_pre_ff"], _tile(rows, 256))
    y = _ffn(h2, x1, w["w_ff1"], w["w_ff2"], w["g_post_ff"], _tile(rows, 512), 512)
    return y.reshape(batch, seq, D_MODEL)


def _prepare(meta_tokens, w_in, w_proj_hg, w_proj_att, w_out, w_ff1, w_ff2, g_pre_mix, g_post_mix,
             g_pre_ff, g_post_ff, lb_logits, hg_out_gain, attn_sink):
    w = {
        "w_in": w_in[0].astype(BF16),
        "w_proj_hg": w_proj_hg[0].astype(BF16),
        "w_proj_att": w_proj_att[0].astype(BF16),
        "w_out": w_out[0].astype(BF16),
        "w_ff1": w_ff1[0].astype(BF16),
        "w_ff2": w_ff2[0].astype(BF16),
        "g_pre_mix": g_pre_mix[0].reshape(1, D_MODEL).astype(F32),
        "g_post_mix": g_post_mix[0].reshape(1, D_MODEL).astype(F32),
        "g_pre_ff": g_pre_ff[0].reshape(1, D_MODEL).astype(F32),
        "g_post_ff": g_post_ff[0].reshape(1, D_MODEL).astype(F32),
        "lb_logits": lb_logits.reshape(2, 2 * HG_KDIM).astype(F32),
        "hg_out_gain": hg_out_gain[0].reshape(1, HG_WIDTH).astype(F32),
        "attn_sink": attn_sink[0].astype(F32),
    }
    p1_m, lf_m, p2_m, _ = _in_proj(meta_tokens.astype(F32), w["g_pre_mix"], w["w_in"], w["lb_logits"], N_META)
    return w, (p1_m, lf_m, p2_m)


def kernel(x_prompt, x_sample, meta_tokens, w_in, w_proj_hg, w_proj_att, w_out, w_ff1, w_ff2, g_pre_mix, g_post_mix, g_pre_ff, g_post_ff, lb_logits, hg_out_gain, attn_sink):
    w, meta_proj = _prepare(meta_tokens, w_in, w_proj_hg, w_proj_att, w_out, w_ff1, w_ff2, g_pre_mix,
                            g_post_mix, g_pre_ff, g_post_ff, lb_logits, hg_out_gain, attn_sink)
    y_prompt = _trunk(x_prompt, meta_proj, w, x_prompt.shape[0], x_prompt.shape[1])
    y_sample = _trunk(x_sample, meta_proj, w, x_sample.shape[0], x_sample.shape[1])
    return (y_prompt, y_sample)
```

```python
import functools
import math

import jax
import jax.numpy as jnp
from jax import lax
from jax.experimental import pallas as pl
from jax.experimental.pallas import tpu as pltpu

F32 = jnp.float32
BF16 = jnp.bfloat16

D_MODEL = 2048
N_META = 16
HG_HEADS = 8
HG_DK = 128
HG_DV = 128
HG_KDIM = HG_HEADS * HG_DK
HG_WIDTH = HG_HEADS * HG_DV
HG_CHUNK = 64
N_Q_HEADS = 16
N_KV_HEADS = 4
HEAD_DIM = 64
GROUP = N_Q_HEADS // N_KV_HEADS
ATT_WIDTH = N_Q_HEADS * HEAD_DIM
KV_WIDTH = N_KV_HEADS * HEAD_DIM
WINDOW = 128
ATT_BLOCK = 128
D_FF = 4 * D_MODEL
EPS = 1e-6
IN_COLS = 3 * HG_KDIM + 2 * HG_WIDTH + ATT_WIDTH + 2 * KV_WIDTH + 2 * D_MODEL

VMEM_LIMIT_BYTES = 56 * 1024 * 1024

IN_TILE = 512
T_Q_END = HG_KDIM // IN_TILE
T_I_END = T_Q_END + HG_WIDTH // IN_TILE
T_F_END = T_I_END + 2 * HG_KDIM // IN_TILE
T_G_END = T_F_END + HG_WIDTH // IN_TILE
T_AQ_END = T_G_END + ATT_WIDTH // IN_TILE
T_KV_END = T_AQ_END + 2 * KV_WIDTH // IN_TILE
T_GATE_END = T_KV_END + 2 * D_MODEL // IN_TILE
P1_COLS = HG_KDIM + HG_WIDTH
LF_COLS = 2 * HG_KDIM
P2_COLS = HG_WIDTH + ATT_WIDTH + 2 * KV_WIDTH
PG_COLS = 2 * D_MODEL


def _sigmoid(x):
    return 1.0 / (1.0 + jnp.exp(-x))


def _rms_scale(x):
    return lax.rsqrt(jnp.mean(x * x, axis=-1, keepdims=True) + EPS)


def _in_proj_kernel(x_ref, g_ref, w_ref, lbl_ref, p1_ref, lf_ref, p2_ref, pg_ref, h_ref, *, nsplit):
    j = pl.program_id(1)
    sub = h_ref.shape[0] // nsplit

    @pl.when(j == 0)
    def _():
        x = x_ref[...]
        h_ref[...] = (x * _rms_scale(x) * g_ref[...]).astype(BF16)

    def column_group(cond, out_ref, epilogue):
        @pl.when(cond)
        def _():
            for r in range(nsplit):
                rows = slice(r * sub, (r + 1) * sub)
                acc = jnp.dot(h_ref[rows, :], w_ref[...], preferred_element_type=F32)
                out_ref[rows, :] = epilogue(acc).astype(out_ref.dtype)

    def log_forget(acc):
        logits = lbl_ref[...]
        e = jnp.exp(logits - jnp.max(logits, axis=0, keepdims=True))
        lb = e[0:1] / jnp.sum(e, axis=0, keepdims=True)
        return jnp.log(lb + (1.0 - lb) * _sigmoid(acc))

    column_group(j < T_Q_END, p1_ref, lambda a: a * _sigmoid(a) * (HG_DK ** -0.5))
    column_group((j >= T_Q_END) & (j < T_I_END), p1_ref, lambda a: a)
    column_group((j >= T_I_END) & (j < T_F_END), lf_ref, log_forget)
    column_group((j >= T_F_END) & (j < T_G_END), p2_ref, lambda a: a * _sigmoid(a))
    column_group((j >= T_G_END) & (j < T_AQ_END), p2_ref, lambda a: a * (HEAD_DIM ** -0.5))
    column_group((j >= T_AQ_END) & (j < T_KV_END), p2_ref, lambda a: a)
    column_group(j >= T_KV_END, pg_ref, _sigmoid)


def _in_proj(x, gain, w_bf16, lb_logits2, tm):
    rows = x.shape[0]
    assert rows % tm == 0
    nsplit = 4 if tm % (4 * 128) == 0 else 1

    def clamp(j, lo, hi):
        return jnp.minimum(jnp.maximum(j - lo, 0), hi - lo - 1)

    return pl.pallas_call(
        functools.partial(_in_proj_kernel, nsplit=nsplit),
        grid=(rows // tm, T_GATE_END),
        in_specs=[
            pl.BlockSpec((tm, D_MODEL), lambda i, j: (i, 0)),
            pl.BlockSpec((1, D_MODEL), lambda i, j: (0, 0)),
            pl.BlockSpec((D_MODEL, IN_TILE), lambda i, j: (0, j)),
            pl.BlockSpec((2, IN_TILE), lambda i, j: (0, clamp(j, T_I_END, T_F_END))),
        ],
        out_specs=[
            pl.BlockSpec((tm, IN_TILE), lambda i, j: (i, clamp(j, 0, T_I_END))),
            pl.BlockSpec((tm, IN_TILE), lambda i, j: (i, clamp(j, T_I_END, T_F_END))),
            pl.BlockSpec((tm, IN_TILE), lambda i, j: (i, clamp(j, T_F_END, T_KV_END))),
            pl.BlockSpec((tm, IN_TILE), lambda i, j: (i, clamp(j, T_KV_END, T_GATE_END))),
        ],
        out_shape=[
            jax.ShapeDtypeStruct((rows, P1_COLS), BF16),
            jax.ShapeDtypeStruct((rows, LF_COLS), F32),
            jax.ShapeDtypeStruct((rows, P2_COLS), BF16),
            jax.ShapeDtypeStruct((rows, PG_COLS), BF16),
        ],
        scratch_shapes=[pltpu.VMEM((tm, D_MODEL), BF16)],
        compiler_params=pltpu.CompilerParams(
            dimension_semantics=("arbitrary", "arbitrary"), vmem_limit_bytes=VMEM_LIMIT_BYTES),
        name="in_proj",
    )(x, gain, w_bf16, lb_logits2)


def _tri(n, lower):
    r = lax.broadcasted_iota(jnp.int32, (n, n), 0)
    c = lax.broadcasted_iota(jnp.int32, (n, n), 1)
    return (r >= c) if lower else (r <= c)


def _cumsum_rows(tri_bf16, lf):
    width = lf.shape[1]
    hi = lf.astype(BF16)
    lo = (lf - hi.astype(F32)).astype(BF16)
    r = jnp.dot(tri_bf16, jnp.concatenate([hi, lo], axis=1), preferred_element_type=F32)
    return r[:, :width] + r[:, width:]


_NT = (((1,), (1,)), ((), ()))
_TN = (((0,), (0,)), ((), ()))


def _scan_kernel(qf_ref, vf_ref, lff_ref, qb_ref, vb_ref, lfb_ref, vm_ref, lfm_ref,
                 of_ref, ob_ref, sf_ref, sb_ref, qd_ref, ki_ref, ke_ref, dec_ref, *, nchunks):
    n = pl.program_id(1)
    c_len = HG_CHUNK
    masks = (_tri(c_len, True), _tri(c_len, False))
    tris = tuple(jnp.where(m, 1.0, 0.0).astype(BF16) for m in masks)
    end_rows = (c_len - 1, 0)
    q_refs, v_refs, lf_refs = (qf_ref, qb_ref), (vf_ref, vb_ref), (lff_ref, lfb_ref)
    o_refs, s_refs = (of_ref, ob_ref), (sf_ref, sb_ref)

    @pl.when(n == 0)
    def _():
        sb_ref[...] = jnp.zeros_like(sb_ref)
        tri_m = jnp.where(_tri(N_META, True), 1.0, 0.0).astype(BF16)
        lf = lfm_ref[...]
        cum = _cumsum_rows(tri_m, lf)
        k_end = ((1.0 - jnp.exp(lf)) * jnp.exp(cum[N_META - 1:N_META, :] - cum)).astype(BF16)
        for h in range(HG_HEADS):
            sl = slice(h * HG_DK, (h + 1) * HG_DK)
            sf_ref[h] = lax.dot_general(vm_ref[:, sl], k_end[:, sl], _TN, preferred_element_type=F32)

    for c in range(nchunks):
        rows = slice(c * c_len, (c + 1) * c_len)
        for d in range(2):
            lf = lf_refs[d][rows, :]
            cum = _cumsum_rows(tris[d], lf)
            b_end = cum[end_rows[d]:end_rows[d] + 1, :]
            k = 1.0 - jnp.exp(lf)
            qd_ref[d, rows, :] = (q_refs[d][rows, :].astype(F32) * jnp.exp(cum)).astype(BF16)
            ki_ref[d, rows, :] = (k * jnp.exp(-cum)).astype(BF16)
            ke_ref[d, rows, :] = (k * jnp.exp(b_end - cum)).astype(BF16)
            dec_ref[d, c] = jnp.exp(b_end)

    def body(c, carry):
        chunk = (c, nchunks - 1 - c)
        work = []
        for d in range(2):
            rows = pl.ds(pl.multiple_of(chunk[d] * c_len, c_len), c_len)
            dec = dec_ref[d, chunk[d]]
            for h in range(HG_HEADS):
                sl = slice(h * HG_DK, (h + 1) * HG_DK)
                qd = qd_ref[d, rows, sl]
                v = v_refs[d][rows, sl]
                st = s_refs[d][h]
                scores = lax.dot_general(qd, ki_ref[d, rows, sl], _NT, preferred_element_type=F32)
                inter = lax.dot_general(qd, st.astype(BF16), _NT, preferred_element_type=F32)
                inc = lax.dot_general(v, ke_ref[d, rows, sl], _TN, preferred_element_type=F32)
                s_refs[d][h] = st * dec[:, sl] + inc
                work.append((d, rows, sl, scores, inter, v))
        for d, rows, sl, scores, inter, v in work:
            intra = jnp.dot(jnp.where(masks[d], scores, 0.0).astype(BF16), v, preferred_element_type=F32)
            o_refs[d][rows, sl] = intra + inter
        return carry

    lax.fori_loop(0, nchunks, body, 0)


def _scan(p1, lf, p1_meta, lf_meta, batch, seq, rb):
    rows = batch * seq
    assert seq % rb == 0 and rb % HG_CHUNK == 0
    nb = seq // rb

    def fwd(col):
        return lambda b, n: (b * nb + n, col)

    def bwd(col):
        return lambda b, n: (b * nb + nb - 1 - n, col)

    return pl.pallas_call(
        functools.partial(_scan_kernel, nchunks=rb // HG_CHUNK),
        grid=(batch, nb),
        in_specs=[
            pl.BlockSpec((rb, HG_KDIM), fwd(0)),
            pl.BlockSpec((rb, HG_WIDTH), fwd(1)),
            pl.BlockSpec((rb, HG_KDIM), fwd(0)),
            pl.BlockSpec((rb, HG_KDIM), bwd(0)),
            pl.BlockSpec((rb, HG_WIDTH), bwd(1)),
            pl.BlockSpec((rb, HG_KDIM), bwd(1)),
            pl.BlockSpec((N_META, HG_WIDTH), lambda b, n: (0, 1)),
            pl.BlockSpec((N_META, HG_KDIM), lambda b, n: (0, 0)),
        ],
        out_specs=[
            pl.BlockSpec((rb, HG_WIDTH), fwd(0)),
            pl.BlockSpec((rb, HG_WIDTH), bwd(0)),
        ],
        out_shape=[
            jax.ShapeDtypeStruct((rows, HG_WIDTH), F32),
            jax.ShapeDtypeStruct((rows, HG_WIDTH), F32),
        ],
        scratch_shapes=[
            pltpu.VMEM((HG_HEADS, HG_DV, HG_DK), F32),
            pltpu.VMEM((HG_HEADS, HG_DV, HG_DK), F32),
            pltpu.VMEM((2, rb, HG_KDIM), BF16),
            pltpu.VMEM((2, rb, HG_KDIM), BF16),
            pltpu.VMEM((2, rb, HG_KDIM), BF16),
            pltpu.VMEM((2, rb // HG_CHUNK, 1, HG_KDIM), F32),
        ],
        compiler_params=pltpu.CompilerParams(
            dimension_semantics=("arbitrary", "arbitrary"), vmem_limit_bytes=VMEM_LIMIT_BYTES),
        name="hgrn2_scan",
    )(p1, p1, lf, p1, p1, lf, p1_meta, lf_meta)


N_KEYS = 3 * ATT_BLOCK + N_META


def _alibi_slope(head):
    return 2.0 ** (-8.0 * (head + 1) / N_Q_HEADS)


LANES = 128
HEADS_PER_GROUP = LANES // HEAD_DIM


def _attn_kernel(sink_ref, q_ref, kvc_ref, kvp_ref, kvn_ref, kvm_ref, o_ref, bias_ref, *, seq, qb):
    n = pl.program_id(1)
    nsub = qb // ATT_BLOCK
    neg_inf = -jnp.inf

    @pl.when((pl.program_id(0) == 0) & (n == 0))
    def _():
        u = lax.broadcasted_iota(jnp.int32, (N_KEYS, ATT_BLOCK), 0)
        r = lax.broadcasted_iota(jnp.int32, (N_KEYS, ATT_BLOCK), 1)
        dist = jnp.abs(u - ATT_BLOCK - r)
        is_meta = u >= 3 * ATT_BLOCK
        dist_f = dist.astype(F32)
        for head in range(N_Q_HEADS):
            inside = jnp.where(dist <= WINDOW, -_alibi_slope(head) * dist_f, neg_inf)
            bias_ref[head] = jnp.where(is_meta, 0.0, inside)

    lane = lax.broadcasted_iota(jnp.int32, (N_KEYS, LANES), 1)
    low_half = lane < HEAD_DIM
    for i in range(nsub):
        rows = slice(i * ATT_BLOCK, (i + 1) * ATT_BLOCK)
        kv_prev = kvp_ref[...] if i == 0 else kvc_ref[(i - 1) * ATT_BLOCK:i * ATT_BLOCK, :]
        kv_next = kvn_ref[...] if i == nsub - 1 else kvc_ref[(i + 1) * ATT_BLOCK:(i + 2) * ATT_BLOCK, :]
        kv_all = jnp.concatenate([kv_prev, kvc_ref[rows, :], kv_next, kvm_ref[...]], axis=0)
        pen_prev = jnp.where(n * qb + (i - 1) * ATT_BLOCK < 0, neg_inf, 0.0) if i == 0 else None
        pen_next = jnp.where(n * qb + (i + 1) * ATT_BLOCK >= seq, neg_inf, 0.0) if i == nsub - 1 else None
        for m in range(N_KV_HEADS // HEADS_PER_GROUP):
            kcol = kv_all[:, m * LANES:(m + 1) * LANES]
            vcol = kv_all[:, KV_WIDTH + m * LANES:KV_WIDTH + (m + 1) * LANES]
            kswap = jnp.concatenate([kcol[:, HEAD_DIM:], kcol[:, :HEAD_DIM]], axis=1)
            probs, inv_denoms = [], []
            for e in range(HEADS_PER_GROUP):
                h = m * HEADS_PER_GROUP + e
                k_lo = jnp.where(low_half, kcol if e == 0 else kswap, 0).astype(BF16)
                k_hi = jnp.where(low_half, 0, kswap if e == 0 else kcol).astype(BF16)
                qcols = [q_ref[rows, (2 * h + c) * LANES:(2 * h + c + 1) * LANES] for c in range(2)]
                s_all = lax.dot_general(jnp.concatenate([k_lo, k_hi], axis=0), jnp.concatenate(qcols, axis=0),
                                        _NT, preferred_element_type=F32)
                for g in range(GROUP):
                    head = h * GROUP + g
                    half, c = g % 2, g // 2
                    s = s_all[half * N_KEYS:(half + 1) * N_KEYS, c * ATT_BLOCK:(c + 1) * ATT_BLOCK]
                    bias = bias_ref[head]
                    if pen_prev is not None or pen_next is not None:
                        pieces = [bias[0:ATT_BLOCK], bias[ATT_BLOCK:2 * ATT_BLOCK],
                                  bias[2 * ATT_BLOCK:3 * ATT_BLOCK], bias[3 * ATT_BLOCK:]]
                        if pen_prev is not None:
                            pieces[0] = pieces[0] + pen_prev
                        if pen_next is not None:
                            pieces[2] = pieces[2] + pen_next
                        bias = jnp.concatenate(pieces, axis=0)
                    s = s + bias
                    sink = sink_ref[head]
                    mx = jnp.maximum(jnp.max(s, axis=0, keepdims=True), sink)
                    p = jnp.exp(s - mx)
                    denom = jnp.sum(p, axis=0, keepdims=True) + jnp.exp(sink - mx)
                    probs.append(p.astype(BF16))
                    inv_denoms.append(1.0 / denom)
            o_t = lax.dot_general(vcol, jnp.concatenate(probs, axis=1), _TN, preferred_element_type=F32)
            for e in range(HEADS_PER_GROUP):
                h = m * HEADS_PER_GROUP + e
                for c in range(2):
                    pair = []
                    for half in range(2):
                        idx = e * GROUP + 2 * c + half
                        pair.append(o_t[e * HEAD_DIM:(e + 1) * HEAD_DIM, idx * ATT_BLOCK:(idx + 1) * ATT_BLOCK]
                                    * inv_denoms[idx])
                    out = jnp.concatenate(pair, axis=0).T
                    o_ref[rows, (2 * h + c) * LANES:(2 * h + c + 1) * LANES] = out.astype(BF16)


def _attn(p2, p2_meta, sink, batch, seq, qb):
    rows = batch * seq
    assert seq % qb == 0 and qb % ATT_BLOCK == 0
    nb = seq // qb
    per = qb // ATT_BLOCK
    last = rows // ATT_BLOCK - 1
    q_col = HG_WIDTH // ATT_WIDTH
    kv_col = (HG_WIDTH + ATT_WIDTH) // (2 * KV_WIDTH)

    return pl.pallas_call(
        functools.partial(_attn_kernel, seq=seq, qb=qb),
        grid=(batch, nb),
        in_specs=[
            pl.BlockSpec(memory_space=pltpu.SMEM),
            pl.BlockSpec((qb, ATT_WIDTH), lambda b, n: (b * nb + n, q_col)),
            pl.BlockSpec((qb, 2 * KV_WIDTH), lambda b, n: (b * nb + n, kv_col)),
            pl.BlockSpec((ATT_BLOCK, 2 * KV_WIDTH),
                         lambda b, n: (jnp.maximum((b * nb + n) * per - 1, 0), kv_col)),
            pl.BlockSpec((ATT_BLOCK, 2 * KV_WIDTH),
                         lambda b, n: (jnp.minimum((b * nb + n + 1) * per, last), kv_col)),
            pl.BlockSpec((N_META, 2 * KV_WIDTH), lambda b, n: (0, kv_col)),
        ],
        out_specs=pl.BlockSpec((qb, ATT_WIDTH), lambda b, n: (b * nb + n, 0)),
        out_shape=jax.ShapeDtypeStruct((rows, ATT_WIDTH), BF16),
        scratch_shapes=[pltpu.VMEM((N_Q_HEADS, N_KEYS, ATT_BLOCK), F32)],
        compiler_params=pltpu.CompilerParams(
            dimension_semantics=("arbitrary", "arbitrary"), vmem_limit_bytes=VMEM_LIMIT_BYTES),
        name="window_attn",
    )(sink, p2, p2, p2, p2, p2_meta)


def _merge_kernel(of_ref, ob_ref, g_ref, att_ref, ga_ref, gb_ref, x_ref, wa_ref, wb_ref, wo_ref,
                  hgain_ref, gpost_ref, gpre_ref, x1_ref, h2_ref):
    o = of_ref[...] + ob_ref[...]
    parts = []
    for h in range(HG_HEADS):
        oh = o[:, h * HG_DV:(h + 1) * HG_DV]
        parts.append(oh * _rms_scale(oh))
    a = jnp.concatenate(parts, axis=1) * hgain_ref[...] * g_ref[...].astype(F32)
    y = ga_ref[...].astype(F32) * jnp.dot(a.astype(BF16), wa_ref[...], preferred_element_type=F32)
    y = y + gb_ref[...].astype(F32) * jnp.dot(att_ref[...], wb_ref[...], preferred_element_type=F32)
    z = jnp.dot(y.astype(BF16), wo_ref[...], preferred_element_type=F32)
    x1 = x_ref[...] + z * _rms_scale(z) * gpost_ref[...]
    x1_ref[...] = x1
    h2_ref[...] = (x1 * _rms_scale(x1) * gpre_ref[...]).astype(BF16)


def _merge(o_f, o_b, p2, att, pg, x, wa, wb, wo, hgain, gpost, gpre, tm):
    rows = x.shape[0]
    assert rows % tm == 0
    row = lambda c: (lambda i: (i, c))
    const = lambda i: (0, 0)
    single = pl.Buffered(1)
    return pl.pallas_call(
        _merge_kernel,
        grid=(rows // tm,),
        in_specs=[
            pl.BlockSpec((tm, HG_WIDTH), row(0)),
            pl.BlockSpec((tm, HG_WIDTH), row(0)),
            pl.BlockSpec((tm, HG_WIDTH), row(0)),
            pl.BlockSpec((tm, ATT_WIDTH), row(0)),
            pl.BlockSpec((tm, D_MODEL), row(0)),
            pl.BlockSpec((tm, D_MODEL), row(1)),
            pl.BlockSpec((tm, D_MODEL), row(0)),
            pl.BlockSpec((HG_WIDTH, D_MODEL), const, pipeline_mode=single),
            pl.BlockSpec((ATT_WIDTH, D_MODEL), const, pipeline_mode=single),
            pl.BlockSpec((D_MODEL, D_MODEL), const, pipeline_mode=single),
            pl.BlockSpec((1, HG_WIDTH), const),
            pl.BlockSpec((1, D_MODEL), const),
            pl.BlockSpec((1, D_MODEL), const),
        ],
        out_specs=[
            pl.BlockSpec((tm, D_MODEL), row(0)),
            pl.BlockSpec((tm, D_MODEL), row(0)),
        ],
        out_shape=[
            jax.ShapeDtypeStruct((rows, D_MODEL), F32),
            jax.ShapeDtypeStruct((rows, D_MODEL), BF16),
        ],
        compiler_params=pltpu.CompilerParams(
            dimension_semantics=("arbitrary",), vmem_limit_bytes=VMEM_LIMIT_BYTES),
        name="merge_out",
    )(o_f, o_b, p2, att, pg, pg, x, wa, wb, wo, hgain, gpost, gpre)


FF_UP_TILE = 1024
FF_DOWN_TILE = 256


def _ffn_kernel(h_ref, x_ref, w1_ref, w2_ref, g_ref, y_ref, u_ref, *, n_up):
    j = pl.program_id(1)

    @pl.when(j < n_up)
    def _():
        u = jnp.maximum(jnp.dot(h_ref[...], w1_ref[...], preferred_element_type=F32), 0.0)
        u_ref[:, pl.ds(pl.multiple_of(j * FF_UP_TILE, FF_UP_TILE), FF_UP_TILE)] = (u * u).astype(BF16)

    @pl.when(j >= n_up)
    def _():
        col = pl.multiple_of((j - n_up) * FF_DOWN_TILE, FF_DOWN_TILE)
        y_ref[:, pl.ds(col, FF_DOWN_TILE)] = jnp.dot(u_ref[...], w2_ref[...], preferred_element_type=F32)

    @pl.when(j == pl.num_programs(1) - 1)
    def _():
        ff = y_ref[...]
        y_ref[...] = x_ref[...] + ff * _rms_scale(ff) * g_ref[...]


def _ffn(h2, x1, w1, w2, gpost, tm):
    rows = x1.shape[0]
    assert rows % tm == 0
    n_up = D_FF // FF_UP_TILE
    n_down = D_MODEL // FF_DOWN_TILE
    return pl.pallas_call(
        functools.partial(_ffn_kernel, n_up=n_up),
        grid=(rows // tm, n_up + n_down),
        in_specs=[
            pl.BlockSpec((tm, D_MODEL), lambda i, j: (i, 0)),
            pl.BlockSpec((tm, D_MODEL), lambda i, j: (i, 0)),
            pl.BlockSpec((D_MODEL, FF_UP_TILE), lambda i, j: (0, jnp.minimum(j, n_up - 1))),
            pl.BlockSpec((D_FF, FF_DOWN_TILE), lambda i, j: (0, jnp.maximum(j - n_up, 0))),
            pl.BlockSpec((1, D_MODEL), lambda i, j: (0, 0)),
        ],
        out_specs=pl.BlockSpec((tm, D_MODEL), lambda i, j: (i, 0)),
        out_shape=jax.ShapeDtypeStruct((rows, D_MODEL), F32),
        scratch_shapes=[pltpu.VMEM((tm, D_FF), BF16)],
        compiler_params=pltpu.CompilerParams(
            dimension_semantics=("arbitrary", "arbitrary"), vmem_limit_bytes=VMEM_LIMIT_BYTES),
        name="ffn",
    )(h2, x1, w1, w2, gpost)


def _tile(n, pref):
    return pref if n % pref == 0 else math.gcd(n, pref)


def _trunk(x, meta_proj, w, batch, seq):
    rows = batch * seq
    xf = x.reshape(rows, D_MODEL)
    p1, lf, p2, pg = _in_proj(xf, w["g_pre_mix"], w["w_in"], w["lb_logits"], _tile(rows, 1024))
    p1_m, lf_m, p2_m = meta_proj
    o_f, o_b = _scan(p1, lf, p1_m, lf_m, batch, seq, _tile(seq, 512))
    att = _attn(p2, p2_m, w["attn_sink"], batch, seq, _tile(seq, 512))
    x1, h2 = _merge(o_f, o_b, p2, att, pg, xf, w["w_proj_hg"], w["w_proj_att"], w["w_out"],
                    w["hg_out_gain"], w["g_post_mix"], w["g_pre_ff"], _tile(rows, 256))
    y = _ffn(h2, x1, w["w_ff1"], w["w_ff2"], w["g_post_ff"], _tile(rows, 512))
    return y.reshape(batch, seq, D_MODEL)


def _prepare(meta_tokens, w_in, w_proj_hg, w_proj_att, w_out, w_ff1, w_ff2, g_pre_mix, g_post_mix,
             g_pre_ff, g_post_ff, lb_logits, hg_out_gain, attn_sink):
    w = {
        "w_in": w_in[0].astype(BF16),
        "w_proj_hg": w_proj_hg[0].astype(BF16),
        "w_proj_att": w_proj_att[0].astype(BF16),
        "w_out": w_out[0].astype(BF16),
        "w_ff1": w_ff1[0].astype(BF16),
        "w_ff2": w_ff2[0].astype(BF16),
        "g_pre_mix": g_pre_mix[0].reshape(1, D_MODEL).astype(F32),
        "g_post_mix": g_post_mix[0].reshape(1, D_MODEL).astype(F32),
        "g_pre_ff": g_pre_ff[0].reshape(1, D_MODEL).astype(F32),
        "g_post_ff": g_post_ff[0].reshape(1, D_MODEL).astype(F32),
        "lb_logits": lb_logits.reshape(2, 2 * HG_KDIM).astype(F32),
        "hg_out_gain": hg_out_gain[0].reshape(1, HG_WIDTH).astype(F32),
        "attn_sink": attn_sink[0].astype(F32),
    }
    p1_m, lf_m, p2_m, _ = _in_proj(meta_tokens.astype(F32), w["g_pre_mix"], w["w_in"], w["lb_logits"], N_META)
    return w, (p1_m, lf_m, p2_m)


def kernel(x_prompt, x_sample, meta_tokens, w_in, w_proj_hg, w_proj_att, w_out, w_ff1, w_ff2, g_pre_mix, g_post_mix, g_pre_ff, g_post_ff, lb_logits, hg_out_gain, attn_sink):
    w, meta_proj = _prepare(meta_tokens, w_in, w_proj_hg, w_proj_att, w_out, w_ff1, w_ff2, g_pre_mix,
                            g_post_mix, g_pre_ff, g_post_ff, lb_logits, hg_out_gain, attn_sink)
    y_prompt = _trunk(x_prompt, meta_proj, w, x_prompt.shape[0], x_prompt.shape[1])
    y_sample = _trunk(x_sample, meta_proj, w, x_sample.shape[0], x_sample.shape[1])
    return (y_prompt, y_sample)
```

```python
import functools
import math

import jax
import jax.numpy as jnp
from jax import lax
from jax.experimental import pallas as pl
from jax.experimental.pallas import tpu as pltpu

F32 = jnp.float32
BF16 = jnp.bfloat16

D_MODEL = 2048
N_META = 16
HG_HEADS = 8
HG_DK = 128
HG_DV = 128
HG_KDIM = HG_HEADS * HG_DK
HG_WIDTH = HG_HEADS * HG_DV
HG_CHUNK = 64
N_Q_HEADS = 16
N_KV_HEADS = 4
HEAD_DIM = 64
GROUP = N_Q_HEADS // N_KV_HEADS
ATT_WIDTH = N_Q_HEADS * HEAD_DIM
KV_WIDTH = N_KV_HEADS * HEAD_DIM
WINDOW = 128
ATT_BLOCK = 128
D_FF = 4 * D_MODEL
EPS = 1e-6
IN_COLS = 3 * HG_KDIM + 2 * HG_WIDTH + ATT_WIDTH + 2 * KV_WIDTH + 2 * D_MODEL

VMEM_LIMIT_BYTES = 56 * 1024 * 1024

IN_TILE = 1024
GATE_COLS = 2 * D_MODEL
IN_ROTATE = IN_COLS - GATE_COLS
T_GATE_END = GATE_COLS // IN_TILE
T_Q = T_GATE_END
T_I = T_Q + 1
T_F = T_I + 1
T_G = T_F + 2
T_AQ = T_G + 1
T_KV = T_AQ + 1
N_IN_TILES = T_KV + 1
P_COLS = IN_COLS - 2 * HG_KDIM
PB_Q, PB_I, PB_G, PB_AQ = 4, 5, 6, 7
PB_KV = (GATE_COLS + 4 * IN_TILE) // (2 * KV_WIDTH)
LF_COLS = 2 * HG_KDIM


def _sigmoid(x):
    return 1.0 / (1.0 + jnp.exp(-x))


def _rms_scale(x):
    return lax.rsqrt(jnp.mean(x * x, axis=-1, keepdims=True) + EPS)


def _in_proj_kernel(x_ref, g_ref, w_ref, lbl_ref, p_ref, lf_ref, h_ref, *, nsplit):
    j = pl.program_id(1)
    sub = h_ref.shape[0] // nsplit

    @pl.when(j == 0)
    def _():
        x = x_ref[...]
        h_ref[...] = (x * _rms_scale(x) * g_ref[...]).astype(BF16)

    def column_group(cond, out_ref, epilogue, width=IN_TILE):
        @pl.when(cond)
        def _():
            for r in range(nsplit):
                rows = slice(r * sub, (r + 1) * sub)
                acc = jnp.dot(h_ref[rows, :], w_ref[:, :width], preferred_element_type=F32)
                out_ref[rows, :width] = epilogue(acc).astype(out_ref.dtype)

    def log_forget(acc):
        logits = lbl_ref[...]
        e = jnp.exp(logits - jnp.max(logits, axis=0, keepdims=True))
        lb = e[0:1] / jnp.sum(e, axis=0, keepdims=True)
        return jnp.log(lb + (1.0 - lb) * _sigmoid(acc))

    silu_scale = jnp.where(j == T_Q, HG_DK ** -0.5, 1.0)
    linear_scale = jnp.where(j == T_AQ, HEAD_DIM ** -0.5, 1.0)
    column_group(j < T_GATE_END, p_ref, _sigmoid)
    column_group((j == T_Q) | (j == T_G), p_ref, lambda a: a * _sigmoid(a) * silu_scale)
    column_group((j == T_I) | (j == T_AQ), p_ref, lambda a: a * linear_scale)
    column_group((j >= T_F) & (j < T_G), lf_ref, log_forget)
    column_group(j == T_KV, p_ref, lambda a: a, width=2 * KV_WIDTH)


def _in_proj(x, gain, w_rot_bf16, lb_logits2, tm):
    rows = x.shape[0]
    assert rows % tm == 0
    nsplit = 4 if tm % (4 * 128) == 0 else 1

    def p_tile(j):
        return jnp.where(j < T_F, j, jnp.where(j < T_G, T_F - 1, j - 2))

    def f_tile(j):
        return jnp.minimum(jnp.maximum(j - T_F, 0), 1)

    return pl.pallas_call(
        functools.partial(_in_proj_kernel, nsplit=nsplit),
        grid=(rows // tm, N_IN_TILES),
        in_specs=[
            pl.BlockSpec((tm, D_MODEL), lambda i, j: (i, 0)),
            pl.BlockSpec((1, D_MODEL), lambda i, j: (0, 0)),
            pl.BlockSpec((D_MODEL, IN_TILE), lambda i, j: (0, j)),
            pl.BlockSpec((2, IN_TILE), lambda i, j: (0, f_tile(j))),
        ],
        out_specs=[
            pl.BlockSpec((tm, IN_TILE), lambda i, j: (i, p_tile(j))),
            pl.BlockSpec((tm, IN_TILE), lambda i, j: (i, f_tile(j))),
        ],
        out_shape=[
            jax.ShapeDtypeStruct((rows, P_COLS), BF16),
            jax.ShapeDtypeStruct((rows, LF_COLS), F32),
        ],
        scratch_shapes=[pltpu.VMEM((tm, D_MODEL), BF16)],
        compiler_params=pltpu.CompilerParams(
            dimension_semantics=("arbitrary", "arbitrary"), vmem_limit_bytes=VMEM_LIMIT_BYTES),
        name="in_proj",
    )(x, gain, w_rot_bf16, lb_logits2)


def _tri(n, lower):
    r = lax.broadcasted_iota(jnp.int32, (n, n), 0)
    c = lax.broadcasted_iota(jnp.int32, (n, n), 1)
    return (r >= c) if lower else (r <= c)


def _cumsum_rows(tri_bf16, lf):
    width = lf.shape[1]
    hi = lf.astype(BF16)
    lo = (lf - hi.astype(F32)).astype(BF16)
    r = jnp.dot(tri_bf16, jnp.concatenate([hi, lo], axis=1), preferred_element_type=F32)
    return r[:, :width] + r[:, width:]


_NT = (((1,), (1,)), ((), ()))
_TN = (((0,), (0,)), ((), ()))


def _scan_kernel(qf_ref, vf_ref, lff_ref, qb_ref, vb_ref, lfb_ref, vm_ref, lfm_ref,
                 of_ref, ob_ref, sf_ref, sb_ref, qd_ref, ki_ref, ke_ref, dec_ref, *, nchunks):
    n = pl.program_id(1)
    c_len = HG_CHUNK
    masks = (_tri(c_len, True), _tri(c_len, False))
    tris = tuple(jnp.where(m, 1.0, 0.0).astype(BF16) for m in masks)
    end_rows = (c_len - 1, 0)
    q_refs, v_refs, lf_refs = (qf_ref, qb_ref), (vf_ref, vb_ref), (lff_ref, lfb_ref)
    o_refs, s_refs = (of_ref, ob_ref), (sf_ref, sb_ref)

    @pl.when(n == 0)
    def _():
        sb_ref[...] = jnp.zeros_like(sb_ref)
        tri_m = jnp.where(_tri(N_META, True), 1.0, 0.0).astype(BF16)
        lf = lfm_ref[...]
        cum = _cumsum_rows(tri_m, lf)
        k_end = ((1.0 - jnp.exp(lf)) * jnp.exp(cum[N_META - 1:N_META, :] - cum)).astype(BF16)
        for h in range(HG_HEADS):
            sl = slice(h * HG_DK, (h + 1) * HG_DK)
            sf_ref[h] = lax.dot_general(vm_ref[:, sl], k_end[:, sl], _TN, preferred_element_type=F32)

    for c in range(nchunks):
        rows = slice(c * c_len, (c + 1) * c_len)
        for d in range(2):
            lf = lf_refs[d][rows, :]
            cum = _cumsum_rows(tris[d], lf)
            b_end = cum[end_rows[d]:end_rows[d] + 1, :]
            k = 1.0 - jnp.exp(lf)
            qd_ref[d, rows, :] = (q_refs[d][rows, :].astype(F32) * jnp.exp(cum)).astype(BF16)
            ki_ref[d, rows, :] = (k * jnp.exp(-cum)).astype(BF16)
            ke_ref[d, rows, :] = (k * jnp.exp(b_end - cum)).astype(BF16)
            dec_ref[d, c] = jnp.exp(b_end)

    def body(c, carry):
        chunk = (c, nchunks - 1 - c)
        work = []
        for d in range(2):
            rows = pl.ds(pl.multiple_of(chunk[d] * c_len, c_len), c_len)
            dec = dec_ref[d, chunk[d]]
            for h in range(HG_HEADS):
                sl = slice(h * HG_DK, (h + 1) * HG_DK)
                qd = qd_ref[d, rows, sl]
                v = v_refs[d][rows, sl]
                st = s_refs[d][h]
                scores = lax.dot_general(qd, ki_ref[d, rows, sl], _NT, preferred_element_type=F32)
                inter = lax.dot_general(qd, st.astype(BF16), _NT, preferred_element_type=F32)
                inc = lax.dot_general(v, ke_ref[d, rows, sl], _TN, preferred_element_type=F32)
                s_refs[d][h] = st * dec[:, sl] + inc
                work.append((d, rows, sl, scores, inter, v))
        for d, rows, sl, scores, inter, v in work:
            intra = jnp.dot(jnp.where(masks[d], scores, 0.0).astype(BF16), v, preferred_element_type=F32)
            o_refs[d][rows, sl] = intra + inter
        return carry

    lax.fori_loop(0, nchunks, body, 0, unroll=2 if nchunks % 2 == 0 else 1)


def _scan(p, lf, p_meta, lf_meta, batch, seq, rb):
    rows = batch * seq
    assert seq % rb == 0 and rb % HG_CHUNK == 0
    nb = seq // rb

    def fwd(col):
        return lambda b, n: (b * nb + n, col)

    def bwd(col):
        return lambda b, n: (b * nb + nb - 1 - n, col)

    return pl.pallas_call(
        functools.partial(_scan_kernel, nchunks=rb // HG_CHUNK),
        grid=(batch, nb),
        in_specs=[
            pl.BlockSpec((rb, HG_KDIM), fwd(PB_Q)),
            pl.BlockSpec((rb, HG_WIDTH), fwd(PB_I)),
            pl.BlockSpec((rb, HG_KDIM), fwd(0)),
            pl.BlockSpec((rb, HG_KDIM), bwd(PB_Q)),
            pl.BlockSpec((rb, HG_WIDTH), bwd(PB_I)),
            pl.BlockSpec((rb, HG_KDIM), bwd(1)),
            pl.BlockSpec((N_META, HG_WIDTH), lambda b, n: (0, PB_I)),
            pl.BlockSpec((N_META, HG_KDIM), lambda b, n: (0, 0)),
        ],
        out_specs=[
            pl.BlockSpec((rb, HG_WIDTH), fwd(0)),
            pl.BlockSpec((rb, HG_WIDTH), bwd(0)),
        ],
        out_shape=[
            jax.ShapeDtypeStruct((rows, HG_WIDTH), F32),
            jax.ShapeDtypeStruct((rows, HG_WIDTH), F32),
        ],
        scratch_shapes=[
            pltpu.VMEM((HG_HEADS, HG_DV, HG_DK), F32),
            pltpu.VMEM((HG_HEADS, HG_DV, HG_DK), F32),
            pltpu.VMEM((2, rb, HG_KDIM), BF16),
            pltpu.VMEM((2, rb, HG_KDIM), BF16),
            pltpu.VMEM((2, rb, HG_KDIM), BF16),
            pltpu.VMEM((2, rb // HG_CHUNK, 1, HG_KDIM), F32),
        ],
        compiler_params=pltpu.CompilerParams(
            dimension_semantics=("arbitrary", "arbitrary"), vmem_limit_bytes=VMEM_LIMIT_BYTES),
        name="hgrn2_scan",
    )(p, p, lf, p, p, lf, p_meta, lf_meta)


N_KEYS = 3 * ATT_BLOCK + N_META


def _alibi_slope(head):
    return 2.0 ** (-8.0 * (head + 1) / N_Q_HEADS)


LANES = 128
HEADS_PER_GROUP = LANES // HEAD_DIM


def _attn_kernel(sink_ref, q_ref, kvc_ref, kvp_ref, kvn_ref, kvm_ref, o_ref, bias_ref, *, seq, qb):
    n = pl.program_id(1)
    nsub = qb // ATT_BLOCK
    neg_inf = -jnp.inf

    @pl.when((pl.program_id(0) == 0) & (n == 0))
    def _():
        u = lax.broadcasted_iota(jnp.int32, (N_KEYS, ATT_BLOCK), 0)
        r = lax.broadcasted_iota(jnp.int32, (N_KEYS, ATT_BLOCK), 1)
        dist = jnp.abs(u - ATT_BLOCK - r)
        is_meta = u >= 3 * ATT_BLOCK
        dist_f = dist.astype(F32)
        for head in range(N_Q_HEADS):
            inside = jnp.where(dist <= WINDOW, -_alibi_slope(head) * dist_f, neg_inf)
            bias_ref[head] = jnp.where(is_meta, 0.0, inside)

    lane = lax.broadcasted_iota(jnp.int32, (N_KEYS, LANES), 1)
    low_half = lane < HEAD_DIM
    for i in range(nsub):
        rows = slice(i * ATT_BLOCK, (i + 1) * ATT_BLOCK)
        kv_prev = kvp_ref[...] if i == 0 else kvc_ref[(i - 1) * ATT_BLOCK:i * ATT_BLOCK, :]
        kv_next = kvn_ref[...] if i == nsub - 1 else kvc_ref[(i + 1) * ATT_BLOCK:(i + 2) * ATT_BLOCK, :]
        kv_all = jnp.concatenate([kv_prev, kvc_ref[rows, :], kv_next, kvm_ref[...]], axis=0)
        pen_prev = jnp.where(n * qb + (i - 1) * ATT_BLOCK < 0, neg_inf, 0.0) if i == 0 else None
        pen_next = jnp.where(n * qb + (i + 1) * ATT_BLOCK >= seq, neg_inf, 0.0) if i == nsub - 1 else None
        for m in range(N_KV_HEADS // HEADS_PER_GROUP):
            kcol = kv_all[:, m * LANES:(m + 1) * LANES]
            vcol = kv_all[:, KV_WIDTH + m * LANES:KV_WIDTH + (m + 1) * LANES]
            kswap = jnp.concatenate([kcol[:, HEAD_DIM:], kcol[:, :HEAD_DIM]], axis=1)
            probs, inv_denoms = [], []
            for e in range(HEADS_PER_GROUP):
                h = m * HEADS_PER_GROUP + e
                k_lo = jnp.where(low_half, kcol if e == 0 else kswap, 0).astype(BF16)
                k_hi = jnp.where(low_half, 0, kswap if e == 0 else kcol).astype(BF16)
                qcols = [q_ref[rows, (2 * h + c) * LANES:(2 * h + c + 1) * LANES] for c in range(2)]
                s_all = lax.dot_general(jnp.concatenate([k_lo, k_hi], axis=0), jnp.concatenate(qcols, axis=0),
                                        _NT, preferred_element_type=F32)
                for g in range(GROUP):
                    head = h * GROUP + g
                    half, c = g % 2, g // 2
                    s = s_all[half * N_KEYS:(half + 1) * N_KEYS, c * ATT_BLOCK:(c + 1) * ATT_BLOCK]
                    bias = bias_ref[head]
                    if pen_prev is not None or pen_next is not None:
                        pieces = [bias[0:ATT_BLOCK], bias[ATT_BLOCK:2 * ATT_BLOCK],
                                  bias[2 * ATT_BLOCK:3 * ATT_BLOCK], bias[3 * ATT_BLOCK:]]
                        if pen_prev is not None:
                            pieces[0] = pieces[0] + pen_prev
                        if pen_next is not None:
                            pieces[2] = pieces[2] + pen_next
                        bias = jnp.concatenate(pieces, axis=0)
                    s = s + bias
                    sink = sink_ref[head]
                    mx = jnp.maximum(jnp.max(s, axis=0, keepdims=True), sink)
                    p = jnp.exp(s - mx)
                    denom = jnp.sum(p, axis=0, keepdims=True) + jnp.exp(sink - mx)
                    probs.append(p.astype(BF16))
                    inv_denoms.append(1.0 / denom)
            o_t = lax.dot_general(vcol, jnp.concatenate(probs, axis=1), _TN, preferred_element_type=F32)
            for e in range(HEADS_PER_GROUP):
                h = m * HEADS_PER_GROUP + e
                for c in range(2):
                    pair = []
                    for half in range(2):
                        idx = e * GROUP + 2 * c + half
                        pair.append(o_t[e * HEAD_DIM:(e + 1) * HEAD_DIM, idx * ATT_BLOCK:(idx + 1) * ATT_BLOCK]
                                    * inv_denoms[idx])
                    out = jnp.concatenate(pair, axis=0).T
                    o_ref[rows, (2 * h + c) * LANES:(2 * h + c + 1) * LANES] = out.astype(BF16)


def _attn(p, p_meta, sink, batch, seq, qb):
    rows = batch * seq
    assert seq % qb == 0 and qb % ATT_BLOCK == 0
    nb = seq // qb
    per = qb // ATT_BLOCK
    last = rows // ATT_BLOCK - 1
    q_col, kv_col = PB_AQ, PB_KV

    return pl.pallas_call(
        functools.partial(_attn_kernel, seq=seq, qb=qb),
        grid=(batch, nb),
        in_specs=[
            pl.BlockSpec(memory_space=pltpu.SMEM),
            pl.BlockSpec((qb, ATT_WIDTH), lambda b, n: (b * nb + n, q_col)),
            pl.BlockSpec((qb, 2 * KV_WIDTH), lambda b, n: (b * nb + n, kv_col)),
            pl.BlockSpec((ATT_BLOCK, 2 * KV_WIDTH),
                         lambda b, n: (jnp.maximum((b * nb + n) * per - 1, 0), kv_col)),
            pl.BlockSpec((ATT_BLOCK, 2 * KV_WIDTH),
                         lambda b, n: (jnp.minimum((b * nb + n + 1) * per, last), kv_col)),
            pl.BlockSpec((N_META, 2 * KV_WIDTH), lambda b, n: (0, kv_col)),
        ],
        out_specs=pl.BlockSpec((qb, ATT_WIDTH), lambda b, n: (b * nb + n, 0)),
        out_shape=jax.ShapeDtypeStruct((rows, ATT_WIDTH), BF16),
        scratch_shapes=[pltpu.VMEM((N_Q_HEADS, N_KEYS, ATT_BLOCK), F32)],
        compiler_params=pltpu.CompilerParams(
            dimension_semantics=("arbitrary", "arbitrary"), vmem_limit_bytes=VMEM_LIMIT_BYTES),
        name="window_attn",
    )(sink, p, p, p, p, p_meta)


def _merge_kernel(of_ref, ob_ref, g_ref, att_ref, ga_ref, gb_ref, x_ref, wa_ref, wb_ref, wo_ref,
                  hgain_ref, gpost_ref, gpre_ref, x1_ref, h2_ref, *, nsplit):
    sub = x_ref.shape[0] // nsplit
    for r in range(nsplit):
        rows = slice(r * sub, (r + 1) * sub)
        y = gb_ref[rows, :].astype(F32) * jnp.dot(att_ref[rows, :], wb_ref[...], preferred_element_type=F32)
        o = of_ref[rows, :] + ob_ref[rows, :]
        parts = []
        for h in range(HG_HEADS):
            oh = o[:, h * HG_DV:(h + 1) * HG_DV]
            parts.append(oh * _rms_scale(oh))
        a = jnp.concatenate(parts, axis=1) * hgain_ref[...] * g_ref[rows, :].astype(F32)
        y = y + ga_ref[rows, :].astype(F32) * jnp.dot(a.astype(BF16), wa_ref[...], preferred_element_type=F32)
        z = jnp.dot(y.astype(BF16), wo_ref[...], preferred_element_type=F32)
        x1 = x_ref[rows, :] + z * _rms_scale(z) * gpost_ref[...]
        x1_ref[rows, :] = x1
        h2_ref[rows, :] = (x1 * _rms_scale(x1) * gpre_ref[...]).astype(BF16)


def _merge(o_f, o_b, p, att, x, wa, wb, wo, hgain, gpost, gpre, tm):
    rows = x.shape[0]
    assert rows % tm == 0
    row = lambda c: (lambda i: (i, c))
    const = lambda i: (0, 0)
    single = pl.Buffered(1)
    nsplit = 2 if tm % 256 == 0 else 1
    return pl.pallas_call(
        functools.partial(_merge_kernel, nsplit=nsplit),
        grid=(rows // tm,),
        in_specs=[
            pl.BlockSpec((tm, HG_WIDTH), row(0)),
            pl.BlockSpec((tm, HG_WIDTH), row(0)),
            pl.BlockSpec((tm, HG_WIDTH), row(PB_G)),
            pl.BlockSpec((tm, ATT_WIDTH), row(0)),
            pl.BlockSpec((tm, D_MODEL), row(0)),
            pl.BlockSpec((tm, D_MODEL), row(1)),
            pl.BlockSpec((tm, D_MODEL), row(0)),
            pl.BlockSpec((HG_WIDTH, D_MODEL), const, pipeline_mode=single),
            pl.BlockSpec((ATT_WIDTH, D_MODEL), const, pipeline_mode=single),
            pl.BlockSpec((D_MODEL, D_MODEL), const, pipeline_mode=single),
            pl.BlockSpec((1, HG_WIDTH), const),
            pl.BlockSpec((1, D_MODEL), const),
            pl.BlockSpec((1, D_MODEL), const),
        ],
        out_specs=[
            pl.BlockSpec((tm, D_MODEL), row(0)),
            pl.BlockSpec((tm, D_MODEL), row(0)),
        ],
        out_shape=[
            jax.ShapeDtypeStruct((rows, D_MODEL), F32),
            jax.ShapeDtypeStruct((rows, D_MODEL), BF16),
        ],
        compiler_params=pltpu.CompilerParams(
            dimension_semantics=("arbitrary",), vmem_limit_bytes=VMEM_LIMIT_BYTES),
        name="merge_out",
    )(o_f, o_b, p, att, p, p, x, wa, wb, wo, hgain, gpost, gpre)


FF_UP_TILE = 1024
FF_DOWN_TILE = 512


def _ffn_up_kernel(h_ref, w1_ref, u_ref, *, nsplit):
    sub = h_ref.shape[0] // nsplit
    for r in range(nsplit):
        rows = slice(r * sub, (r + 1) * sub)
        u = jnp.maximum(jnp.dot(h_ref[rows, :], w1_ref[...], preferred_element_type=F32), 0.0)
        u_ref[rows, :] = (u * u).astype(BF16)


def _ffn_up(h2, w1, tm):
    rows = h2.shape[0]
    assert rows % tm == 0
    nsplit = 4 if tm % (4 * 128) == 0 else 1
    return pl.pallas_call(
        functools.partial(_ffn_up_kernel, nsplit=nsplit),
        grid=(rows // tm, D_FF // FF_UP_TILE),
        in_specs=[
            pl.BlockSpec((tm, D_MODEL), lambda i, j: (i, 0)),
            pl.BlockSpec((D_MODEL, FF_UP_TILE), lambda i, j: (0, j)),
        ],
        out_specs=pl.BlockSpec((tm, FF_UP_TILE), lambda i, j: (i, j)),
        out_shape=jax.ShapeDtypeStruct((rows, D_FF), BF16),
        compiler_params=pltpu.CompilerParams(
            dimension_semantics=("arbitrary", "arbitrary"), vmem_limit_bytes=VMEM_LIMIT_BYTES),
        name="ffn_up",
    )(h2, w1)


def _ffn_down_kernel(u_ref, x_ref, w2_ref, g_ref, y_ref):
    j = pl.program_id(1)
    col = pl.multiple_of(j * FF_DOWN_TILE, FF_DOWN_TILE)
    y_ref[:, pl.ds(col, FF_DOWN_TILE)] = jnp.dot(u_ref[...], w2_ref[...], preferred_element_type=F32)

    @pl.when(j == pl.num_programs(1) - 1)
    def _():
        ff = y_ref[...]
        y_ref[...] = x_ref[...] + ff * _rms_scale(ff) * g_ref[...]


def _ffn_down(u, x1, w2, gpost, tm):
    rows = x1.shape[0]
    assert rows % tm == 0
    return pl.pallas_call(
        _ffn_down_kernel,
        grid=(rows // tm, D_MODEL // FF_DOWN_TILE),
        in_specs=[
            pl.BlockSpec((tm, D_FF), lambda i, j: (i, 0)),
            pl.BlockSpec((tm, D_MODEL), lambda i, j: (i, 0)),
            pl.BlockSpec((D_FF, FF_DOWN_TILE), lambda i, j: (0, j)),
            pl.BlockSpec((1, D_MODEL), lambda i, j: (0, 0)),
        ],
        out_specs=pl.BlockSpec((tm, D_MODEL), lambda i, j: (i, 0)),
        out_shape=jax.ShapeDtypeStruct((rows, D_MODEL), F32),
        compiler_params=pltpu.CompilerParams(
            dimension_semantics=("arbitrary", "arbitrary"), vmem_limit_bytes=VMEM_LIMIT_BYTES),
        name="ffn_down",
    )(u, x1, w2, gpost)


def _tile(n, pref):
    return pref if n % pref == 0 else math.gcd(n, pref)


def _trunk(x, meta_proj, w, batch, seq):
    rows = batch * seq
    xf = x.reshape(rows, D_MODEL)
    p, lf = _in_proj(xf, w["g_pre_mix"], w["w_in"], w["lb_logits"], _tile(rows, 1024))
    p_m, lf_m = meta_proj
    o_f, o_b = _scan(p, lf, p_m, lf_m, batch, seq, _tile(seq, 512))
    att = _attn(p, p_m, w["attn_sink"], batch, seq, _tile(seq, 512))
    x1, h2 = _merge(o_f, o_b, p, att, xf, w["w_proj_hg"], w["w_proj_att"], w["w_out"],
                    w["hg_out_gain"], w["g_post_mix"], w["g_pre_ff"], _tile(rows, 256))
    u = _ffn_up(h2, w["w_ff1"], _tile(rows, 2048))
    y = _ffn_down(u, x1, w["w_ff2"], w["g_post_ff"], _tile(rows, 512))
    return y.reshape(batch, seq, D_MODEL)


def _prepare(meta_tokens, w_in, w_proj_hg, w_proj_att, w_out, w_ff1, w_ff2, g_pre_mix, g_post_mix,
             g_pre_ff, g_post_ff, lb_logits, hg_out_gain, attn_sink):
    w = {
        "w_in": jnp.concatenate([w_in[0, :, IN_ROTATE:], w_in[0, :, :IN_ROTATE]], axis=1).astype(BF16),
        "w_proj_hg": w_proj_hg[0].astype(BF16),
        "w_proj_att": w_proj_att[0].astype(BF16),
        "w_out": w_out[0].astype(BF16),
        "w_ff1": w_ff1[0].astype(BF16),
        "w_ff2": w_ff2[0].astype(BF16),
        "g_pre_mix": g_pre_mix[0].reshape(1, D_MODEL).astype(F32),
        "g_post_mix": g_post_mix[0].reshape(1, D_MODEL).astype(F32),
        "g_pre_ff": g_pre_ff[0].reshape(1, D_MODEL).astype(F32),
        "g_post_ff": g_post_ff[0].reshape(1, D_MODEL).astype(F32),
        "lb_logits": lb_logits.reshape(2, 2 * HG_KDIM).astype(F32),
        "hg_out_gain": hg_out_gain[0].reshape(1, HG_WIDTH).astype(F32),
        "attn_sink": attn_sink[0].astype(F32),
    }
    meta_proj = _in_proj(meta_tokens.astype(F32), w["g_pre_mix"], w["w_in"], w["lb_logits"], N_META)
    return w, meta_proj


def kernel(x_prompt, x_sample, meta_tokens, w_in, w_proj_hg, w_proj_att, w_out, w_ff1, w_ff2, g_pre_mix, g_post_mix, g_pre_ff, g_post_ff, lb_logits, hg_out_gain, attn_sink):
    w, meta_proj = _prepare(meta_tokens, w_in, w_proj_hg, w_proj_att, w_out, w_ff1, w_ff2, g_pre_mix,
                            g_post_mix, g_pre_ff, g_post_ff, lb_logits, hg_out_gain, attn_sink)
    y_prompt = _trunk(x_prompt, meta_proj, w, x_prompt.shape[0], x_prompt.shape[1])
    y_sample = _trunk(x_sample, meta_proj, w, x_sample.shape[0], x_sample.shape[1])
    return (y_prompt, y_sample)
```

```python
import functools
import math

import jax
import jax.numpy as jnp
import numpy as np
from jax import lax
from jax.experimental import pallas as pl
from jax.experimental.pallas import tpu as pltpu

F32 = jnp.float32
BF16 = jnp.bfloat16

D_MODEL = 2048
N_META = 16
HG_HEADS = 8
HG_DK = 128
HG_DV = 128
HG_KDIM = HG_HEADS * HG_DK
HG_WIDTH = HG_HEADS * HG_DV
HG_CHUNK = 64
N_Q_HEADS = 16
N_KV_HEADS = 4
HEAD_DIM = 64
GROUP = N_Q_HEADS // N_KV_HEADS
ATT_WIDTH = N_Q_HEADS * HEAD_DIM
KV_WIDTH = N_KV_HEADS * HEAD_DIM
WINDOW = 128
ATT_BLOCK = 128
D_FF = 4 * D_MODEL
EPS = 1e-6
IN_COLS = 3 * HG_KDIM + 2 * HG_WIDTH + ATT_WIDTH + 2 * KV_WIDTH + 2 * D_MODEL

VMEM_LIMIT_BYTES = 56 * 1024 * 1024

IN_TILE = 1024
W_F_TILE = (HG_KDIM + HG_WIDTH) // IN_TILE
N_W_TILES = pl.cdiv(IN_COLS, IN_TILE)
N_P_TILES = N_W_TILES - 2
P_COLS = IN_COLS - 2 * HG_KDIM
LAST_TILE_COLS = P_COLS - (N_P_TILES - 1) * IN_TILE
LF_COLS = 2 * HG_KDIM
PB_Q, PB_I, PB_G, PB_AQ = 0, 1, 2, 3
PB_KV = 4 * IN_TILE // (2 * KV_WIDTH)
P_GATE_START = 4 * IN_TILE + 2 * KV_WIDTH
PB_GATE_TILES = tuple(range(4, N_P_TILES))


def _activation_table():
    t = np.zeros((3, N_P_TILES * IN_TILE), np.float32)
    t[2, :P_GATE_START] = 1.0
    t[0, P_GATE_START:P_COLS] = 1.0
    for block, scale in ((PB_Q, HG_DK ** -0.5), (PB_G, 1.0)):
        t[1, block * IN_TILE:(block + 1) * IN_TILE] = scale
        t[2, block * IN_TILE:(block + 1) * IN_TILE] = 0.0
    t[2, PB_AQ * IN_TILE:(PB_AQ + 1) * IN_TILE] = HEAD_DIM ** -0.5
    return t


def _sigmoid(x):
    return 1.0 / (1.0 + jnp.exp(-x))


def _rms_scale(x):
    return lax.rsqrt(jnp.mean(x * x, axis=-1, keepdims=True) + EPS)


def _row_blocks(tm):
    if tm % (4 * 128) != 0:
        return [slice(0, tm)]
    return [slice(r * tm // 4, (r + 1) * tm // 4) for r in range(4)]


def _in_proj_kernel(x_ref, g_ref, w_ref, act_ref, lbl_ref, p_ref, lf_ref, h_ref):
    j = pl.program_id(1)
    row_blocks = _row_blocks(h_ref.shape[0])

    @pl.when(j == 0)
    def _():
        x = x_ref[...]
        h_ref[...] = (x * _rms_scale(x) * g_ref[...]).astype(BF16)

    def tile(cond, out_ref, epilogue, width):
        @pl.when(cond)
        def _():
            for rows in row_blocks:
                acc = jnp.dot(h_ref[rows, :], w_ref[:, :width], preferred_element_type=F32)
                out_ref[rows, :width] = epilogue(acc, width).astype(out_ref.dtype)

    def activation(acc, width):
        sig = _sigmoid(acc)
        c_gate, c_silu, c_linear = (act_ref[k:k + 1, :width] for k in range(3))
        return acc * (c_silu * sig + c_linear) + c_gate * sig

    def log_forget(acc, width):
        logits = lbl_ref[...]
        e = jnp.exp(logits - jnp.max(logits, axis=0, keepdims=True))
        lb = e[0:1] / jnp.sum(e, axis=0, keepdims=True)
        return jnp.log(lb + (1.0 - lb) * _sigmoid(acc))

    tile(j < N_P_TILES - 1, p_ref, activation, IN_TILE)
    tile(j == N_P_TILES - 1, p_ref, activation, LAST_TILE_COLS)
    tile(j >= N_P_TILES, lf_ref, log_forget, IN_TILE)


def _in_proj(x, gain, w_bf16, act_table, lb_logits2, tm):
    rows = x.shape[0]
    assert rows % tm == 0

    def w_tile(j):
        return jnp.where(j < W_F_TILE, j, jnp.where(j < N_P_TILES, j + 2, j - N_P_TILES + W_F_TILE))

    def p_tile(j):
        return jnp.minimum(j, N_P_TILES - 1)

    def f_tile(j):
        return jnp.maximum(j - N_P_TILES, 0)

    return pl.pallas_call(
        _in_proj_kernel,
        grid=(rows // tm, N_W_TILES),
        in_specs=[
            pl.BlockSpec((tm, D_MODEL), lambda i, j: (i, 0)),
            pl.BlockSpec((1, D_MODEL), lambda i, j: (0, 0)),
            pl.BlockSpec((D_MODEL, IN_TILE), lambda i, j: (0, w_tile(j))),
            pl.BlockSpec((3, IN_TILE), lambda i, j: (0, p_tile(j))),
            pl.BlockSpec((2, IN_TILE), lambda i, j: (0, f_tile(j))),
        ],
        out_specs=[
            pl.BlockSpec((tm, IN_TILE), lambda i, j: (i, p_tile(j))),
            pl.BlockSpec((tm, IN_TILE), lambda i, j: (i, f_tile(j))),
        ],
        out_shape=[
            jax.ShapeDtypeStruct((rows, P_COLS), BF16),
            jax.ShapeDtypeStruct((rows, LF_COLS), F32),
        ],
        scratch_shapes=[pltpu.VMEM((tm, D_MODEL), BF16)],
        compiler_params=pltpu.CompilerParams(
            dimension_semantics=("arbitrary", "arbitrary"), vmem_limit_bytes=VMEM_LIMIT_BYTES),
        name="in_proj",
    )(x, gain, w_bf16, act_table, lb_logits2)


def _tri(n, lower):
    r = lax.broadcasted_iota(jnp.int32, (n, n), 0)
    c = lax.broadcasted_iota(jnp.int32, (n, n), 1)
    return (r >= c) if lower else (r <= c)


def _cumsum_rows(tri_bf16, lf):
    width = lf.shape[1]
    hi = lf.astype(BF16)
    lo = (lf - hi.astype(F32)).astype(BF16)
    r = jnp.dot(tri_bf16, jnp.concatenate([hi, lo], axis=1), preferred_element_type=F32)
    return r[:, :width] + r[:, width:]


_NT = (((1,), (1,)), ((), ()))
_TN = (((0,), (0,)), ((), ()))


def _scan_kernel(qf_ref, vf_ref, lff_ref, qb_ref, vb_ref, lfb_ref, vm_ref, lfm_ref,
                 of_ref, ob_ref, sf_ref, sb_ref, qd_ref, ki_ref, ke_ref, dec_ref, *, nchunks):
    n = pl.program_id(1)
    c_len = HG_CHUNK
    masks = (_tri(c_len, True), _tri(c_len, False))
    tris = tuple(jnp.where(m, 1.0, 0.0).astype(BF16) for m in masks)
    end_rows = (c_len - 1, 0)
    q_refs, v_refs, lf_refs = (qf_ref, qb_ref), (vf_ref, vb_ref), (lff_ref, lfb_ref)
    o_refs, s_refs = (of_ref, ob_ref), (sf_ref, sb_ref)

    @pl.when(n == 0)
    def _():
        sb_ref[...] = jnp.zeros_like(sb_ref)
        tri_m = jnp.where(_tri(N_META, True), 1.0, 0.0).astype(BF16)
        lf = lfm_ref[...]
        cum = _cumsum_rows(tri_m, lf)
        k_end = ((1.0 - jnp.exp(lf)) * jnp.exp(cum[N_META - 1:N_META, :] - cum)).astype(BF16)
        for h in range(HG_HEADS):
            sl = slice(h * HG_DK, (h + 1) * HG_DK)
            sf_ref[h] = lax.dot_general(vm_ref[:, sl], k_end[:, sl], _TN, preferred_element_type=F32)

    for c in range(nchunks):
        rows = slice(c * c_len, (c + 1) * c_len)
        for d in range(2):
            lf = lf_refs[d][rows, :]
            cum = _cumsum_rows(tris[d], lf)
            b_end = cum[end_rows[d]:end_rows[d] + 1, :]
            k = 1.0 - jnp.exp(lf)
            qd_ref[d, rows, :] = (q_refs[d][rows, :].astype(F32) * jnp.exp(cum)).astype(BF16)
            ki_ref[d, rows, :] = (k * jnp.exp(-cum)).astype(BF16)
            ke_ref[d, rows, :] = (k * jnp.exp(b_end - cum)).astype(BF16)
            dec_ref[d, c] = jnp.exp(b_end)

    def body(c, carry):
        chunk = (c, nchunks - 1 - c)
        work = []
        for d in range(2):
            rows = pl.ds(pl.multiple_of(chunk[d] * c_len, c_len), c_len)
            dec = dec_ref[d, chunk[d]]
            for h in range(HG_HEADS):
                sl = slice(h * HG_DK, (h + 1) * HG_DK)
                qd = qd_ref[d, rows, sl]
                v = v_refs[d][rows, sl]
                st = s_refs[d][h]
                scores = lax.dot_general(qd, ki_ref[d, rows, sl], _NT, preferred_element_type=F32)
                inter = lax.dot_general(qd, st.astype(BF16), _NT, preferred_element_type=F32)
                inc = lax.dot_general(v, ke_ref[d, rows, sl], _TN, preferred_element_type=F32)
                s_refs[d][h] = st * dec[:, sl] + inc
                work.append((d, rows, sl, scores, inter, v))
        for d, rows, sl, scores, inter, v in work:
            intra = jnp.dot(jnp.where(masks[d], scores, 0.0).astype(BF16), v, preferred_element_type=F32)
            o_refs[d][rows, sl] = intra + inter
        return carry

    lax.fori_loop(0, nchunks, body, 0, unroll=2 if nchunks % 2 == 0 else 1)


def _scan(p, lf, p_meta, lf_meta, batch, seq, rb):
    rows = batch * seq
    assert seq % rb == 0 and rb % HG_CHUNK == 0
    nb = seq // rb

    def fwd(col):
        return lambda b, n: (b * nb + n, col)

    def bwd(col):
        return lambda b, n: (b * nb + nb - 1 - n, col)

    return pl.pallas_call(
        functools.partial(_scan_kernel, nchunks=rb // HG_CHUNK),
        grid=(batch, nb),
        in_specs=[
            pl.BlockSpec((rb, HG_KDIM), fwd(PB_Q)),
            pl.BlockSpec((rb, HG_WIDTH), fwd(PB_I)),
            pl.BlockSpec((rb, HG_KDIM), fwd(0)),
            pl.BlockSpec((rb, HG_KDIM), bwd(PB_Q)),
            pl.BlockSpec((rb, HG_WIDTH), bwd(PB_I)),
            pl.BlockSpec((rb, HG_KDIM), bwd(1)),
            pl.BlockSpec((N_META, HG_WIDTH), lambda b, n: (0, PB_I)),
            pl.BlockSpec((N_META, HG_KDIM), lambda b, n: (0, 0)),
        ],
        out_specs=[
            pl.BlockSpec((rb, HG_WIDTH), fwd(0)),
            pl.BlockSpec((rb, HG_WIDTH), bwd(0)),
        ],
        out_shape=[
            jax.ShapeDtypeStruct((rows, HG_WIDTH), F32),
            jax.ShapeDtypeStruct((rows, HG_WIDTH), F32),
        ],
        scratch_shapes=[
            pltpu.VMEM((HG_HEADS, HG_DV, HG_DK), F32),
            pltpu.VMEM((HG_HEADS, HG_DV, HG_DK), F32),
            pltpu.VMEM((2, rb, HG_KDIM), BF16),
            pltpu.VMEM((2, rb, HG_KDIM), BF16),
            pltpu.VMEM((2, rb, HG_KDIM), BF16),
            pltpu.VMEM((2, rb // HG_CHUNK, 1, HG_KDIM), F32),
        ],
        compiler_params=pltpu.CompilerParams(
            dimension_semantics=("arbitrary", "arbitrary"), vmem_limit_bytes=VMEM_LIMIT_BYTES),
        name="hgrn2_scan",
    )(p, p, lf, p, p, lf, p_meta, lf_meta)


N_KEYS = 3 * ATT_BLOCK + N_META


def _alibi_slope(head):
    return 2.0 ** (-8.0 * (head + 1) / N_Q_HEADS)


LANES = 128
HEADS_PER_GROUP = LANES // HEAD_DIM


def _attn_kernel(sink_ref, q_ref, kvc_ref, kvp_ref, kvn_ref, kvm_ref, o_ref, bias_ref, *, seq, qb):
    n = pl.program_id(1)
    nsub = qb // ATT_BLOCK
    neg_inf = -jnp.inf

    @pl.when((pl.program_id(0) == 0) & (n == 0))
    def _():
        u = lax.broadcasted_iota(jnp.int32, (N_KEYS, ATT_BLOCK), 0)
        r = lax.broadcasted_iota(jnp.int32, (N_KEYS, ATT_BLOCK), 1)
        dist = jnp.abs(u - ATT_BLOCK - r)
        is_meta = u >= 3 * ATT_BLOCK
        dist_f = dist.astype(F32)
        for head in range(N_Q_HEADS):
            inside = jnp.where(dist <= WINDOW, -_alibi_slope(head) * dist_f, neg_inf)
            bias_ref[head] = jnp.where(is_meta, 0.0, inside)

    lane = lax.broadcasted_iota(jnp.int32, (N_KEYS, LANES), 1)
    low_half = lane < HEAD_DIM
    for i in range(nsub):
        rows = slice(i * ATT_BLOCK, (i + 1) * ATT_BLOCK)
        kv_prev = kvp_ref[...] if i == 0 else kvc_ref[(i - 1) * ATT_BLOCK:i * ATT_BLOCK, :]
        kv_next = kvn_ref[...] if i == nsub - 1 else kvc_ref[(i + 1) * ATT_BLOCK:(i + 2) * ATT_BLOCK, :]
        kv_all = jnp.concatenate([kv_prev, kvc_ref[rows, :], kv_next, kvm_ref[...]], axis=0)
        pen_prev = jnp.where(n * qb + (i - 1) * ATT_BLOCK < 0, neg_inf, 0.0) if i == 0 else None
        pen_next = jnp.where(n * qb + (i + 1) * ATT_BLOCK >= seq, neg_inf, 0.0) if i == nsub - 1 else None
        for m in range(N_KV_HEADS // HEADS_PER_GROUP):
            kcol = kv_all[:, m * LANES:(m + 1) * LANES]
            vcol = kv_all[:, KV_WIDTH + m * LANES:KV_WIDTH + (m + 1) * LANES]
            kswap = jnp.concatenate([kcol[:, HEAD_DIM:], kcol[:, :HEAD_DIM]], axis=1)
            probs, inv_denoms = [], []
            for e in range(HEADS_PER_GROUP):
                h = m * HEADS_PER_GROUP + e
                k_lo = jnp.where(low_half, kcol if e == 0 else kswap, 0).astype(BF16)
                k_hi = jnp.where(low_half, 0, kswap if e == 0 else kcol).astype(BF16)
                qcols = [q_ref[rows, (2 * h + c) * LANES:(2 * h + c + 1) * LANES] for c in range(2)]
                s_all = lax.dot_general(jnp.concatenate([k_lo, k_hi], axis=0), jnp.concatenate(qcols, axis=0),
                                        _NT, preferred_element_type=F32)
                for g in range(GROUP):
                    head = h * GROUP + g
                    half, c = g % 2, g // 2
                    s = s_all[half * N_KEYS:(half + 1) * N_KEYS, c * ATT_BLOCK:(c + 1) * ATT_BLOCK]
                    bias = bias_ref[head]
                    if pen_prev is not None or pen_next is not None:
                        pieces = [bias[0:ATT_BLOCK], bias[ATT_BLOCK:2 * ATT_BLOCK],
                                  bias[2 * ATT_BLOCK:3 * ATT_BLOCK], bias[3 * ATT_BLOCK:]]
                        if pen_prev is not None:
                            pieces[0] = pieces[0] + pen_prev
                        if pen_next is not None:
                            pieces[2] = pieces[2] + pen_next
                        bias = jnp.concatenate(pieces, axis=0)
                    s = s + bias
                    sink = sink_ref[head]
                    mx = jnp.maximum(jnp.max(s, axis=0, keepdims=True), sink)
                    p = jnp.exp(s - mx)
                    denom = jnp.sum(p, axis=0, keepdims=True) + jnp.exp(sink - mx)
                    probs.append(p.astype(BF16))
                    inv_denoms.append(1.0 / denom)
            o_t = lax.dot_general(vcol, jnp.concatenate(probs, axis=1), _TN, preferred_element_type=F32)
            for e in range(HEADS_PER_GROUP):
                h = m * HEADS_PER_GROUP + e
                for c in range(2):
                    pair = []
                    for half in range(2):
                        idx = e * GROUP + 2 * c + half
                        pair.append(o_t[e * HEAD_DIM:(e + 1) * HEAD_DIM, idx * ATT_BLOCK:(idx + 1) * ATT_BLOCK]
                                    * inv_denoms[idx])
                    out = jnp.concatenate(pair, axis=0).T
                    o_ref[rows, (2 * h + c) * LANES:(2 * h + c + 1) * LANES] = out.astype(BF16)


def _attn(p, p_meta, sink, batch, seq, qb):
    rows = batch * seq
    assert seq % qb == 0 and qb % ATT_BLOCK == 0
    nb = seq // qb
    per = qb // ATT_BLOCK
    last = rows // ATT_BLOCK - 1
    q_col, kv_col = PB_AQ, PB_KV

    return pl.pallas_call(
        functools.partial(_attn_kernel, seq=seq, qb=qb),
        grid=(batch, nb),
        in_specs=[
            pl.BlockSpec(memory_space=pltpu.SMEM),
            pl.BlockSpec((qb, ATT_WIDTH), lambda b, n: (b * nb + n, q_col)),
            pl.BlockSpec((qb, 2 * KV_WIDTH), lambda b, n: (b * nb + n, kv_col)),
            pl.BlockSpec((ATT_BLOCK, 2 * KV_WIDTH),
                         lambda b, n: (jnp.maximum((b * nb + n) * per - 1, 0), kv_col)),
            pl.BlockSpec((ATT_BLOCK, 2 * KV_WIDTH),
                         lambda b, n: (jnp.minimum((b * nb + n + 1) * per, last), kv_col)),
            pl.BlockSpec((N_META, 2 * KV_WIDTH), lambda b, n: (0, kv_col)),
        ],
        out_specs=pl.BlockSpec((qb, ATT_WIDTH), lambda b, n: (b * nb + n, 0)),
        out_shape=jax.ShapeDtypeStruct((rows, ATT_WIDTH), BF16),
        scratch_shapes=[pltpu.VMEM((N_Q_HEADS, N_KEYS, ATT_BLOCK), F32)],
        compiler_params=pltpu.CompilerParams(
            dimension_semantics=("arbitrary", "arbitrary"), vmem_limit_bytes=VMEM_LIMIT_BYTES),
        name="window_attn",
    )(sink, p, p, p, p, p_meta)


def _merge_kernel(of_ref, ob_ref, g_ref, att_ref, gt0_ref, gt1_ref, gt2_ref, gt3_ref, gt4_ref, x_ref,
                  wa_ref, wb_ref, wo_ref, hgain_ref, gpost_ref, gpre_ref, x1_ref, h2_ref, *, nsplit):
    sub = x_ref.shape[0] // nsplit
    half = P_GATE_START % IN_TILE
    for r in range(nsplit):
        rows = slice(r * sub, (r + 1) * sub)
        gate_a = jnp.concatenate([gt0_ref[rows, half:], gt1_ref[rows, :], gt2_ref[rows, :half]], axis=1)
        gate_b = jnp.concatenate([gt2_ref[rows, half:], gt3_ref[rows, :], gt4_ref[rows, :half]], axis=1)
        y = gate_b.astype(F32) * jnp.dot(att_ref[rows, :], wb_ref[...], preferred_element_type=F32)
        o = of_ref[rows, :] + ob_ref[rows, :]
        parts = []
        for h in range(HG_HEADS):
            oh = o[:, h * HG_DV:(h + 1) * HG_DV]
            parts.append(oh * _rms_scale(oh))
        a = jnp.concatenate(parts, axis=1) * hgain_ref[...] * g_ref[rows, :].astype(F32)
        y = y + gate_a.astype(F32) * jnp.dot(a.astype(BF16), wa_ref[...], preferred_element_type=F32)
        z = jnp.dot(y.astype(BF16), wo_ref[...], preferred_element_type=F32)
        x1 = x_ref[rows, :] + z * _rms_scale(z) * gpost_ref[...]
        x1_ref[rows, :] = x1
        h2_ref[rows, :] = (x1 * _rms_scale(x1) * gpre_ref[...]).astype(BF16)


def _merge(o_f, o_b, p, att, x, wa, wb, wo, hgain, gpost, gpre, tm):
    rows = x.shape[0]
    assert rows % tm == 0
    row = lambda c: (lambda i: (i, c))
    const = lambda i: (0, 0)
    single = pl.Buffered(1)
    nsplit = 2 if tm % 256 == 0 else 1
    return pl.pallas_call(
        functools.partial(_merge_kernel, nsplit=nsplit),
        grid=(rows // tm,),
        in_specs=[
            pl.BlockSpec((tm, HG_WIDTH), row(0)),
            pl.BlockSpec((tm, HG_WIDTH), row(0)),
            pl.BlockSpec((tm, HG_WIDTH), row(PB_G)),
            pl.BlockSpec((tm, ATT_WIDTH), row(0)),
            *[pl.BlockSpec((tm, IN_TILE), row(t)) for t in PB_GATE_TILES],
            pl.BlockSpec((tm, D_MODEL), row(0)),
            pl.BlockSpec((HG_WIDTH, D_MODEL), const, pipeline_mode=single),
            pl.BlockSpec((ATT_WIDTH, D_MODEL), const, pipeline_mode=single),
            pl.BlockSpec((D_MODEL, D_MODEL), const, pipeline_mode=single),
            pl.BlockSpec((1, HG_WIDTH), const),
            pl.BlockSpec((1, D_MODEL), const),
            pl.BlockSpec((1, D_MODEL), const),
        ],
        out_specs=[
            pl.BlockSpec((tm, D_MODEL), row(0)),
            pl.BlockSpec((tm, D_MODEL), row(0)),
        ],
        out_shape=[
            jax.ShapeDtypeStruct((rows, D_MODEL), F32),
            jax.ShapeDtypeStruct((rows, D_MODEL), BF16),
        ],
        compiler_params=pltpu.CompilerParams(
            dimension_semantics=("arbitrary",), vmem_limit_bytes=VMEM_LIMIT_BYTES),
        name="merge_out",
    )(o_f, o_b, p, att, *([p] * len(PB_GATE_TILES)), x, wa, wb, wo, hgain, gpost, gpre)


FF_UP_TILE = 1024
FF_DOWN_TILE = 512


def _ffn_up_kernel(h_ref, w1_ref, u_ref, *, nsplit):
    sub = h_ref.shape[0] // nsplit
    for r in range(nsplit):
        rows = slice(r * sub, (r + 1) * sub)
        u = jnp.maximum(jnp.dot(h_ref[rows, :], w1_ref[...], preferred_element_type=F32), 0.0)
        u_ref[rows, :] = (u * u).astype(BF16)


def _ffn_up(h2, w1, tm):
    rows = h2.shape[0]
    assert rows % tm == 0
    nsplit = 4 if tm % (4 * 128) == 0 else 1
    return pl.pallas_call(
        functools.partial(_ffn_up_kernel, nsplit=nsplit),
        grid=(rows // tm, D_FF // FF_UP_TILE),
        in_specs=[
            pl.BlockSpec((tm, D_MODEL), lambda i, j: (i, 0)),
            pl.BlockSpec((D_MODEL, FF_UP_TILE), lambda i, j: (0, j)),
        ],
        out_specs=pl.BlockSpec((tm, FF_UP_TILE), lambda i, j: (i, j)),
        out_shape=jax.ShapeDtypeStruct((rows, D_FF), BF16),
        compiler_params=pltpu.CompilerParams(
            dimension_semantics=("arbitrary", "arbitrary"), vmem_limit_bytes=VMEM_LIMIT_BYTES),
        name="ffn_up",
    )(h2, w1)


def _ffn_down_kernel(u_ref, x_ref, w2_ref, g_ref, y_ref):
    j = pl.program_id(1)
    col = pl.multiple_of(j * FF_DOWN_TILE, FF_DOWN_TILE)
    y_ref[:, pl.ds(col, FF_DOWN_TILE)] = jnp.dot(u_ref[...], w2_ref[...], preferred_element_type=F32)

    @pl.when(j == pl.num_programs(1) - 1)
    def _():
        ff = y_ref[...]
        y_ref[...] = x_ref[...] + ff * _rms_scale(ff) * g_ref[...]


def _ffn_down(u, x1, w2, gpost, tm):
    rows = x1.shape[0]
    assert rows % tm == 0
    return pl.pallas_call(
        _ffn_down_kernel,
        grid=(rows // tm, D_MODEL // FF_DOWN_TILE),
        in_specs=[
            pl.BlockSpec((tm, D_FF), lambda i, j: (i, 0)),
            pl.BlockSpec((tm, D_MODEL), lambda i, j: (i, 0)),
            pl.BlockSpec((D_FF, FF_DOWN_TILE), lambda i, j: (0, j)),
            pl.BlockSpec((1, D_MODEL), lambda i, j: (0, 0)),
        ],
        out_specs=pl.BlockSpec((tm, D_MODEL), lambda i, j: (i, 0)),
        out_shape=jax.ShapeDtypeStruct((rows, D_MODEL), F32),
        compiler_params=pltpu.CompilerParams(
            dimension_semantics=("arbitrary", "arbitrary"), vmem_limit_bytes=VMEM_LIMIT_BYTES),
        name="ffn_down",
    )(u, x1, w2, gpost)


def _tile(n, pref):
    return pref if n % pref == 0 else math.gcd(n, pref)


def _trunk(x, meta_proj, w, batch, seq):
    rows = batch * seq
    xf = x.reshape(rows, D_MODEL)
    p, lf = _in_proj(xf, w["g_pre_mix"], w["w_in"], w["act_table"], w["lb_logits"], _tile(rows, 1024))
    p_m, lf_m = meta_proj
    o_f, o_b = _scan(p, lf, p_m, lf_m, batch, seq, _tile(seq, 512))
    att = _attn(p, p_m, w["attn_sink"], batch, seq, _tile(seq, 512))
    x1, h2 = _merge(o_f, o_b, p, att, xf, w["w_proj_hg"], w["w_proj_att"], w["w_out"],
                    w["hg_out_gain"], w["g_post_mix"], w["g_pre_ff"], _tile(rows, 256))
    u = _ffn_up(h2, w["w_ff1"], _tile(rows, 2048))
    y = _ffn_down(u, x1, w["w_ff2"], w["g_post_ff"], _tile(rows, 512))
    return y.reshape(batch, seq, D_MODEL)


def _prepare(meta_tokens, w_in, w_proj_hg, w_proj_att, w_out, w_ff1, w_ff2, g_pre_mix, g_post_mix,
             g_pre_ff, g_post_ff, lb_logits, hg_out_gain, attn_sink):
    w = {
        "w_in": w_in[0].astype(BF16),
        "act_table": jnp.asarray(_activation_table()),
        "w_proj_hg": w_proj_hg[0].astype(BF16),
        "w_proj_att": w_proj_att[0].astype(BF16),
        "w_out": w_out[0].astype(BF16),
        "w_ff1": w_ff1[0].astype(BF16),
        "w_ff2": w_ff2[0].astype(BF16),
        "g_pre_mix": g_pre_mix[0].reshape(1, D_MODEL).astype(F32),
        "g_post_mix": g_post_mix[0].reshape(1, D_MODEL).astype(F32),
        "g_pre_ff": g_pre_ff[0].reshape(1, D_MODEL).astype(F32),
        "g_post_ff": g_post_ff[0].reshape(1, D_MODEL).astype(F32),
        "lb_logits": lb_logits.reshape(2, 2 * HG_KDIM).astype(F32),
        "hg_out_gain": hg_out_gain[0].reshape(1, HG_WIDTH).astype(F32),
        "attn_sink": attn_sink[0].astype(F32),
    }
    meta_proj = _in_proj(meta_tokens.astype(F32), w["g_pre_mix"], w["w_in"], w["act_table"], w["lb_logits"],
                         N_META)
    return w, meta_proj


def kernel(x_prompt, x_sample, meta_tokens, w_in, w_proj_hg, w_proj_att, w_out, w_ff1, w_ff2, g_pre_mix, g_post_mix, g_pre_ff, g_post_ff, lb_logits, hg_out_gain, attn_sink):
    w, meta_proj = _prepare(meta_tokens, w_in, w_proj_hg, w_proj_att, w_out, w_ff1, w_ff2, g_pre_mix,
                            g_post_mix, g_pre_ff, g_post_ff, lb_logits, hg_out_gain, attn_sink)
    y_prompt = _trunk(x_prompt, meta_proj, w, x_prompt.shape[0], x_prompt.shape[1])
    y_sample = _trunk(x_sample, meta_proj, w, x_sample.shape[0], x_sample.shape[1])
    return (y_prompt, y_sample)
```

```python
import functools
import math

import jax
import jax.numpy as jnp
import numpy as np
from jax import lax
from jax.experimental import pallas as pl
from jax.experimental.pallas import tpu as pltpu

F32 = jnp.float32
BF16 = jnp.bfloat16

D_MODEL = 2048
N_META = 16
HG_HEADS = 8
HG_DK = 128
HG_DV = 128
HG_KDIM = HG_HEADS * HG_DK
HG_WIDTH = HG_HEADS * HG_DV
HG_CHUNK = 64
N_Q_HEADS = 16
N_KV_HEADS = 4
HEAD_DIM = 64
GROUP = N_Q_HEADS // N_KV_HEADS
ATT_WIDTH = N_Q_HEADS * HEAD_DIM
KV_WIDTH = N_KV_HEADS * HEAD_DIM
WINDOW = 128
ATT_BLOCK = 128
D_FF = 4 * D_MODEL
EPS = 1e-6
LOG2E = math.log2(math.e)
IN_COLS = 3 * HG_KDIM + 2 * HG_WIDTH + ATT_WIDTH + 2 * KV_WIDTH + 2 * D_MODEL

VMEM_LIMIT_BYTES = 56 * 1024 * 1024

IN_TILE = 1024
W_F_TILE = (HG_KDIM + HG_WIDTH) // IN_TILE
N_W_TILES = pl.cdiv(IN_COLS, IN_TILE)
N_P_TILES = N_W_TILES - 2
P_COLS = IN_COLS - 2 * HG_KDIM
LAST_TILE_COLS = P_COLS - (N_P_TILES - 1) * IN_TILE
LF_COLS = 2 * HG_KDIM
PB_Q, PB_I, PB_G, PB_AQ = 0, 1, 2, 3
PB_KV = 4 * IN_TILE // (2 * KV_WIDTH)
P_GATE_START = 4 * IN_TILE + 2 * KV_WIDTH
PB_GATE_TILES = tuple(range(4, N_P_TILES))


def _activation_table():
    t = np.zeros((3, N_P_TILES * IN_TILE), np.float32)
    t[2, :P_GATE_START] = 1.0
    t[0, P_GATE_START:P_COLS] = 1.0
    for block, scale in ((PB_Q, HG_DK ** -0.5), (PB_G, 1.0)):
        t[1, block * IN_TILE:(block + 1) * IN_TILE] = scale
        t[2, block * IN_TILE:(block + 1) * IN_TILE] = 0.0
    t[2, PB_AQ * IN_TILE:(PB_AQ + 1) * IN_TILE] = HEAD_DIM ** -0.5 * LOG2E
    return t


def _sigmoid(x):
    return 1.0 / (1.0 + jnp.exp(-x))


def _rms_scale(x):
    return lax.rsqrt(jnp.mean(x * x, axis=-1, keepdims=True) + EPS)


def _row_blocks(tm):
    if tm % (4 * 128) != 0:
        return [slice(0, tm)]
    return [slice(r * tm // 4, (r + 1) * tm // 4) for r in range(4)]


def _serpentine(i, j, n):
    return jnp.where(i % 2 == 0, j, n - 1 - j)


def _in_proj_kernel(x_ref, g_ref, w_ref, act_ref, lbl_ref, p_ref, lf_ref, h_ref):
    first_step = pl.program_id(1) == 0
    j = _serpentine(pl.program_id(0), pl.program_id(1), N_W_TILES)
    row_blocks = _row_blocks(h_ref.shape[0])

    @pl.when(first_step)
    def _():
        x = x_ref[...]
        h_ref[...] = (x * _rms_scale(x) * g_ref[...]).astype(BF16)

    def tile(cond, out_ref, epilogue, width):
        @pl.when(cond)
        def _():
            for rows in row_blocks:
                acc = jnp.dot(h_ref[rows, :], w_ref[:, :width], preferred_element_type=F32)
                out_ref[rows, :width] = epilogue(acc, width).astype(out_ref.dtype)

    def activation(acc, width):
        sig = _sigmoid(acc)
        c_gate, c_silu, c_linear = (act_ref[k:k + 1, :width] for k in range(3))
        return acc * (c_silu * sig + c_linear) + c_gate * sig

    def log_forget(acc, width):
        logits = lbl_ref[...]
        e = jnp.exp(logits - jnp.max(logits, axis=0, keepdims=True))
        lb = e[0:1] / jnp.sum(e, axis=0, keepdims=True)
        return jnp.log(lb + (1.0 - lb) * _sigmoid(acc))

    pure_gates = PB_GATE_TILES[0] + 1
    tile(j < pure_gates, p_ref, activation, IN_TILE)
    tile((j >= pure_gates) & (j < N_P_TILES - 1), p_ref, lambda a, w: _sigmoid(a), IN_TILE)
    tile(j == N_P_TILES - 1, p_ref, lambda a, w: _sigmoid(a), LAST_TILE_COLS)
    tile(j >= N_P_TILES, lf_ref, log_forget, IN_TILE)


def _in_proj(x, gain, w_bf16, act_table, lb_logits2, tm):
    rows = x.shape[0]
    assert rows % tm == 0

    def w_tile(i, j):
        t = _serpentine(i, j, N_W_TILES)
        return jnp.where(t < W_F_TILE, t, jnp.where(t < N_P_TILES, t + 2, t - N_P_TILES + W_F_TILE))

    def p_tile(i, j):
        return jnp.minimum(_serpentine(i, j, N_W_TILES), N_P_TILES - 1)

    def f_tile(i, j):
        return jnp.maximum(_serpentine(i, j, N_W_TILES) - N_P_TILES, 0)

    return pl.pallas_call(
        _in_proj_kernel,
        grid=(rows // tm, N_W_TILES),
        in_specs=[
            pl.BlockSpec((tm, D_MODEL), lambda i, j: (i, 0)),
            pl.BlockSpec((1, D_MODEL), lambda i, j: (0, 0)),
            pl.BlockSpec((D_MODEL, IN_TILE), lambda i, j: (0, w_tile(i, j))),
            pl.BlockSpec((3, IN_TILE), lambda i, j: (0, p_tile(i, j))),
            pl.BlockSpec((2, IN_TILE), lambda i, j: (0, f_tile(i, j))),
        ],
        out_specs=[
            pl.BlockSpec((tm, IN_TILE), lambda i, j: (i, p_tile(i, j))),
            pl.BlockSpec((tm, IN_TILE), lambda i, j: (i, f_tile(i, j))),
        ],
        out_shape=[
            jax.ShapeDtypeStruct((rows, P_COLS), BF16),
            jax.ShapeDtypeStruct((rows, LF_COLS), F32),
        ],
        scratch_shapes=[pltpu.VMEM((tm, D_MODEL), BF16)],
        compiler_params=pltpu.CompilerParams(
            dimension_semantics=("arbitrary", "arbitrary"), vmem_limit_bytes=VMEM_LIMIT_BYTES),
        name="in_proj",
    )(x, gain, w_bf16, act_table, lb_logits2)


def _tri(n, lower):
    r = lax.broadcasted_iota(jnp.int32, (n, n), 0)
    c = lax.broadcasted_iota(jnp.int32, (n, n), 1)
    return (r >= c) if lower else (r <= c)


def _cumsum_rows(tri_bf16, lf):
    width = lf.shape[1]
    hi = lf.astype(BF16)
    lo = (lf - hi.astype(F32)).astype(BF16)
    r = jnp.dot(tri_bf16, jnp.concatenate([hi, lo], axis=1), preferred_element_type=F32)
    return r[:, :width] + r[:, width:]


_NT = (((1,), (1,)), ((), ()))
_TN = (((0,), (0,)), ((), ()))


def _scan_kernel(qf_ref, vf_ref, lff_ref, qb_ref, vb_ref, lfb_ref, vm_ref, lfm_ref,
                 of_ref, ob_ref, sf_ref, sb_ref, qd_ref, ki_ref, ke_ref, dec_ref, *, nchunks):
    n = pl.program_id(1)
    c_len = HG_CHUNK
    masks = (_tri(c_len, True), _tri(c_len, False))
    tris = tuple(jnp.where(m, 1.0, 0.0).astype(BF16) for m in masks)
    end_rows = (c_len - 1, 0)
    q_refs, v_refs, lf_refs = (qf_ref, qb_ref), (vf_ref, vb_ref), (lff_ref, lfb_ref)
    o_refs, s_refs = (of_ref, ob_ref), (sf_ref, sb_ref)

    @pl.when(n == 0)
    def _():
        sb_ref[...] = jnp.zeros_like(sb_ref)
        tri_m = jnp.where(_tri(N_META, True), 1.0, 0.0).astype(BF16)
        lf = lfm_ref[...]
        cum = _cumsum_rows(tri_m, lf)
        k_end = ((1.0 - jnp.exp(lf)) * jnp.exp(cum[N_META - 1:N_META, :] - cum)).astype(BF16)
        for h in range(HG_HEADS):
            sl = slice(h * HG_DK, (h + 1) * HG_DK)
            sf_ref[h] = lax.dot_general(vm_ref[:, sl], k_end[:, sl], _TN, preferred_element_type=F32)

    for c in range(nchunks):
        rows = slice(c * c_len, (c + 1) * c_len)
        for d in range(2):
            lf = lf_refs[d][rows, :]
            cum = _cumsum_rows(tris[d], lf)
            b_end = cum[end_rows[d]:end_rows[d] + 1, :]
            k_inv = (1.0 - jnp.exp(lf)) * jnp.exp(-cum)
            dec = jnp.exp(b_end)
            qd_ref[d, rows, :] = (q_refs[d][rows, :].astype(F32) * jnp.exp(cum)).astype(BF16)
            ki_ref[d, rows, :] = k_inv.astype(BF16)
            ke_ref[d, rows, :] = (k_inv * dec).astype(BF16)
            dec_ref[d, c] = dec

    def body(c, carry):
        chunk = (c, nchunks - 1 - c)
        work = []
        for d in range(2):
            rows = pl.ds(pl.multiple_of(chunk[d] * c_len, c_len), c_len)
            dec = dec_ref[d, chunk[d]]
            for h in range(HG_HEADS):
                sl = slice(h * HG_DK, (h + 1) * HG_DK)
                qd = qd_ref[d, rows, sl]
                v = v_refs[d][rows, sl]
                st = s_refs[d][h]
                scores = lax.dot_general(qd, ki_ref[d, rows, sl], _NT, preferred_element_type=F32)
                inter = lax.dot_general(qd, st.astype(BF16), _NT, preferred_element_type=F32)
                inc = lax.dot_general(v, ke_ref[d, rows, sl], _TN, preferred_element_type=F32)
                s_refs[d][h] = st * dec[:, sl] + inc
                work.append((d, rows, sl, scores, inter, v))
        for d, rows, sl, scores, inter, v in work:
            intra = jnp.dot(jnp.where(masks[d], scores, 0.0).astype(BF16), v, preferred_element_type=F32)
            o_refs[d][rows, sl] = intra + inter
        return carry

    lax.fori_loop(0, nchunks, body, 0, unroll=2 if nchunks % 2 == 0 else 1)


def _scan(p, lf, p_meta, lf_meta, batch, seq, rb):
    rows = batch * seq
    assert seq % rb == 0 and rb % HG_CHUNK == 0
    nb = seq // rb

    def fwd(col):
        return lambda b, n: (b * nb + n, col)

    def bwd(col):
        return lambda b, n: (b * nb + nb - 1 - n, col)

    return pl.pallas_call(
        functools.partial(_scan_kernel, nchunks=rb // HG_CHUNK),
        grid=(batch, nb),
        in_specs=[
            pl.BlockSpec((rb, HG_KDIM), fwd(PB_Q)),
            pl.BlockSpec((rb, HG_WIDTH), fwd(PB_I)),
            pl.BlockSpec((rb, HG_KDIM), fwd(0)),
            pl.BlockSpec((rb, HG_KDIM), bwd(PB_Q)),
            pl.BlockSpec((rb, HG_WIDTH), bwd(PB_I)),
            pl.BlockSpec((rb, HG_KDIM), bwd(1)),
            pl.BlockSpec((N_META, HG_WIDTH), lambda b, n: (0, PB_I)),
            pl.BlockSpec((N_META, HG_KDIM), lambda b, n: (0, 0)),
        ],
        out_specs=[
            pl.BlockSpec((rb, HG_WIDTH), fwd(0)),
            pl.BlockSpec((rb, HG_WIDTH), bwd(0)),
        ],
        out_shape=[
            jax.ShapeDtypeStruct((rows, HG_WIDTH), F32),
            jax.ShapeDtypeStruct((rows, HG_WIDTH), F32),
        ],
        scratch_shapes=[
            pltpu.VMEM((HG_HEADS, HG_DV, HG_DK), F32),
            pltpu.VMEM((HG_HEADS, HG_DV, HG_DK), F32),
            pltpu.VMEM((2, rb, HG_KDIM), BF16),
            pltpu.VMEM((2, rb, HG_KDIM), BF16),
            pltpu.VMEM((2, rb, HG_KDIM), BF16),
            pltpu.VMEM((2, rb // HG_CHUNK, 1, HG_KDIM), F32),
        ],
        compiler_params=pltpu.CompilerParams(
            dimension_semantics=("arbitrary", "arbitrary"), vmem_limit_bytes=VMEM_LIMIT_BYTES),
        name="hgrn2_scan",
    )(p, p, lf, p, p, lf, p_meta, lf_meta)


N_KEYS = 3 * ATT_BLOCK + N_META


def _alibi_slope(head):
    return 2.0 ** (-8.0 * (head + 1) / N_Q_HEADS)


LANES = 128
HEADS_PER_GROUP = LANES // HEAD_DIM


def _attn_kernel(sink_ref, q_ref, kvc_ref, kvp_ref, kvn_ref, kvm_ref, o_ref, bias_ref, *, seq, qb):
    n = pl.program_id(1)
    nsub = qb // ATT_BLOCK
    neg_inf = -jnp.inf

    @pl.when((pl.program_id(0) == 0) & (n == 0))
    def _():
        u = lax.broadcasted_iota(jnp.int32, (N_KEYS, ATT_BLOCK), 0)
        r = lax.broadcasted_iota(jnp.int32, (N_KEYS, ATT_BLOCK), 1)
        dist = jnp.abs(u - ATT_BLOCK - r)
        is_meta = u >= 3 * ATT_BLOCK
        dist_f = dist.astype(F32)
        for head in range(N_Q_HEADS):
            inside = jnp.where(dist <= WINDOW, (-_alibi_slope(head) * LOG2E) * dist_f, neg_inf)
            bias_ref[head] = jnp.where(is_meta, 0.0, inside)

    lane = lax.broadcasted_iota(jnp.int32, (N_KEYS, LANES), 1)
    low_half = lane < HEAD_DIM
    for i in range(nsub):
        rows = slice(i * ATT_BLOCK, (i + 1) * ATT_BLOCK)
        kv_prev = kvp_ref[...] if i == 0 else kvc_ref[(i - 1) * ATT_BLOCK:i * ATT_BLOCK, :]
        kv_next = kvn_ref[...] if i == nsub - 1 else kvc_ref[(i + 1) * ATT_BLOCK:(i + 2) * ATT_BLOCK, :]
        kv_all = jnp.concatenate([kv_prev, kvc_ref[rows, :], kv_next, kvm_ref[...]], axis=0)
        pen_prev = jnp.where(n * qb + (i - 1) * ATT_BLOCK < 0, neg_inf, 0.0) if i == 0 else None
        pen_next = jnp.where(n * qb + (i + 1) * ATT_BLOCK >= seq, neg_inf, 0.0) if i == nsub - 1 else None
        for m in range(N_KV_HEADS // HEADS_PER_GROUP):
            kcol = kv_all[:, m * LANES:(m + 1) * LANES]
            vcol = kv_all[:, KV_WIDTH + m * LANES:KV_WIDTH + (m + 1) * LANES]
            kswap = jnp.concatenate([kcol[:, HEAD_DIM:], kcol[:, :HEAD_DIM]], axis=1)
            probs, inv_denoms = [], []
            for e in range(HEADS_PER_GROUP):
                h = m * HEADS_PER_GROUP + e
                k_lo = jnp.where(low_half, kcol if e == 0 else kswap, 0).astype(BF16)
                k_hi = jnp.where(low_half, 0, kswap if e == 0 else kcol).astype(BF16)
                qcols = [q_ref[rows, (2 * h + c) * LANES:(2 * h + c + 1) * LANES] for c in range(2)]
                s_all = lax.dot_general(jnp.concatenate([k_lo, k_hi], axis=0), jnp.concatenate(qcols, axis=0),
                                        _NT, preferred_element_type=F32)
                for g in range(GROUP):
                    head = h * GROUP + g
                    half, c = g % 2, g // 2
                    s = s_all[half * N_KEYS:(half + 1) * N_KEYS, c * ATT_BLOCK:(c + 1) * ATT_BLOCK]
                    bias = bias_ref[head]
                    if pen_prev is not None or pen_next is not None:
                        pieces = [bias[0:ATT_BLOCK], bias[ATT_BLOCK:2 * ATT_BLOCK],
                                  bias[2 * ATT_BLOCK:3 * ATT_BLOCK], bias[3 * ATT_BLOCK:]]
                        if pen_prev is not None:
                            pieces[0] = pieces[0] + pen_prev
                        if pen_next is not None:
                            pieces[2] = pieces[2] + pen_next
                        bias = jnp.concatenate(pieces, axis=0)
                    s = s + bias
                    sink = sink_ref[head] * LOG2E
                    mx = jnp.maximum(jnp.max(s, axis=0, keepdims=True), sink)
                    p = jnp.exp2(s - mx)
                    denom = jnp.sum(p, axis=0, keepdims=True) + jnp.exp2(sink - mx)
                    probs.append(p.astype(BF16))
                    inv_denoms.append(1.0 / denom)
            o_t = lax.dot_general(vcol, jnp.concatenate(probs, axis=1), _TN, preferred_element_type=F32)
            for e in range(HEADS_PER_GROUP):
                h = m * HEADS_PER_GROUP + e
                for c in range(2):
                    pair = []
                    for half in range(2):
                        idx = e * GROUP + 2 * c + half
                        pair.append(o_t[e * HEAD_DIM:(e + 1) * HEAD_DIM, idx * ATT_BLOCK:(idx + 1) * ATT_BLOCK]
                                    * inv_denoms[idx])
                    out = jnp.concatenate(pair, axis=0).T
                    o_ref[rows, (2 * h + c) * LANES:(2 * h + c + 1) * LANES] = out.astype(BF16)


def _attn(p, p_meta, sink, batch, seq, qb):
    rows = batch * seq
    assert seq % qb == 0 and qb % ATT_BLOCK == 0
    nb = seq // qb
    per = qb // ATT_BLOCK
    last = rows // ATT_BLOCK - 1
    q_col, kv_col = PB_AQ, PB_KV

    return pl.pallas_call(
        functools.partial(_attn_kernel, seq=seq, qb=qb),
        grid=(batch, nb),
        in_specs=[
            pl.BlockSpec(memory_space=pltpu.SMEM),
            pl.BlockSpec((qb, ATT_WIDTH), lambda b, n: (b * nb + n, q_col)),
            pl.BlockSpec((qb, 2 * KV_WIDTH), lambda b, n: (b * nb + n, kv_col)),
            pl.BlockSpec((ATT_BLOCK, 2 * KV_WIDTH),
                         lambda b, n: (jnp.maximum((b * nb + n) * per - 1, 0), kv_col)),
            pl.BlockSpec((ATT_BLOCK, 2 * KV_WIDTH),
                         lambda b, n: (jnp.minimum((b * nb + n + 1) * per, last), kv_col)),
            pl.BlockSpec((N_META, 2 * KV_WIDTH), lambda b, n: (0, kv_col)),
        ],
        out_specs=pl.BlockSpec((qb, ATT_WIDTH), lambda b, n: (b * nb + n, 0)),
        out_shape=jax.ShapeDtypeStruct((rows, ATT_WIDTH), BF16),
        scratch_shapes=[pltpu.VMEM((N_Q_HEADS, N_KEYS, ATT_BLOCK), F32)],
        compiler_params=pltpu.CompilerParams(
            dimension_semantics=("arbitrary", "arbitrary"), vmem_limit_bytes=VMEM_LIMIT_BYTES),
        name="window_attn",
    )(sink, p, p, p, p, p_meta)


def _merge_kernel(of_ref, ob_ref, g_ref, att_ref, gt0_ref, gt1_ref, gt2_ref, gt3_ref, gt4_ref, x_ref,
                  wa_ref, wb_ref, wo_ref, hgain_ref, gpost_ref, gpre_ref, x1_ref, h2_ref, *, nsplit):
    sub = x_ref.shape[0] // nsplit
    half = P_GATE_START % IN_TILE
    for r in range(nsplit):
        rows = slice(r * sub, (r + 1) * sub)
        gate_a = jnp.concatenate([gt0_ref[rows, half:], gt1_ref[rows, :], gt2_ref[rows, :half]], axis=1)
        gate_b = jnp.concatenate([gt2_ref[rows, half:], gt3_ref[rows, :], gt4_ref[rows, :half]], axis=1)
        y = gate_b.astype(F32) * jnp.dot(att_ref[rows, :], wb_ref[...], preferred_element_type=F32)
        o = of_ref[rows, :] + ob_ref[rows, :]
        parts = []
        for h in range(HG_HEADS):
            oh = o[:, h * HG_DV:(h + 1) * HG_DV]
            parts.append(oh * _rms_scale(oh))
        a = jnp.concatenate(parts, axis=1) * hgain_ref[...] * g_ref[rows, :].astype(F32)
        y = y + gate_a.astype(F32) * jnp.dot(a.astype(BF16), wa_ref[...], preferred_element_type=F32)
        z = jnp.dot(y.astype(BF16), wo_ref[...], preferred_element_type=F32)
        x1 = x_ref[rows, :] + z * _rms_scale(z) * gpost_ref[...]
        x1_ref[rows, :] = x1
        h2_ref[rows, :] = (x1 * _rms_scale(x1) * gpre_ref[...]).astype(BF16)


def _merge(o_f, o_b, p, att, x, wa, wb, wo, hgain, gpost, gpre, tm):
    rows = x.shape[0]
    assert rows % tm == 0
    row = lambda c: (lambda i: (i, c))
    const = lambda i: (0, 0)
    single = pl.Buffered(1)
    nsplit = 2 if tm % 256 == 0 else 1
    return pl.pallas_call(
        functools.partial(_merge_kernel, nsplit=nsplit),
        grid=(rows // tm,),
        in_specs=[
            pl.BlockSpec((tm, HG_WIDTH), row(0)),
            pl.BlockSpec((tm, HG_WIDTH), row(0)),
            pl.BlockSpec((tm, HG_WIDTH), row(PB_G)),
            pl.BlockSpec((tm, ATT_WIDTH), row(0)),
            *[pl.BlockSpec((tm, IN_TILE), row(t)) for t in PB_GATE_TILES],
            pl.BlockSpec((tm, D_MODEL), row(0)),
            pl.BlockSpec((HG_WIDTH, D_MODEL), const, pipeline_mode=single),
            pl.BlockSpec((ATT_WIDTH, D_MODEL), const, pipeline_mode=single),
            pl.BlockSpec((D_MODEL, D_MODEL), const, pipeline_mode=single),
            pl.BlockSpec((1, HG_WIDTH), const),
            pl.BlockSpec((1, D_MODEL), const),
            pl.BlockSpec((1, D_MODEL), const),
        ],
        out_specs=[
            pl.BlockSpec((tm, D_MODEL), row(0)),
            pl.BlockSpec((tm, D_MODEL), row(0)),
        ],
        out_shape=[
            jax.ShapeDtypeStruct((rows, D_MODEL), F32),
            jax.ShapeDtypeStruct((rows, D_MODEL), BF16),
        ],
        compiler_params=pltpu.CompilerParams(
            dimension_semantics=("arbitrary",), vmem_limit_bytes=VMEM_LIMIT_BYTES),
        name="merge_out",
    )(o_f, o_b, p, att, *([p] * len(PB_GATE_TILES)), x, wa, wb, wo, hgain, gpost, gpre)


FF_UP_TILE = 1024
FF_DOWN_TILE = 512


def _ffn_up_kernel(h_ref, w1_ref, u_ref, *, nsplit):
    sub = h_ref.shape[0] // nsplit
    for r in range(nsplit):
        rows = slice(r * sub, (r + 1) * sub)
        u = jnp.maximum(jnp.dot(h_ref[rows, :], w1_ref[...], preferred_element_type=F32), 0.0)
        u_ref[rows, :] = (u * u).astype(BF16)


def _ffn_up(h2, w1, tm):
    rows = h2.shape[0]
    assert rows % tm == 0
    nsplit = 4 if tm % (4 * 128) == 0 else 1
    return pl.pallas_call(
        functools.partial(_ffn_up_kernel, nsplit=nsplit),
        grid=(rows // tm, D_FF // FF_UP_TILE),
        in_specs=[
            pl.BlockSpec((tm, D_MODEL), lambda i, j: (i, 0)),
            pl.BlockSpec((D_MODEL, FF_UP_TILE), lambda i, j: (0, j)),
        ],
        out_specs=pl.BlockSpec((tm, FF_UP_TILE), lambda i, j: (i, j)),
        out_shape=jax.ShapeDtypeStruct((rows, D_FF), BF16),
        compiler_params=pltpu.CompilerParams(
            dimension_semantics=("arbitrary", "arbitrary"), vmem_limit_bytes=VMEM_LIMIT_BYTES),
        name="ffn_up",
    )(h2, w1)


def _ffn_down_kernel(u_ref, x_ref, w2_ref, g_ref, y_ref):
    tile = _serpentine(pl.program_id(0), pl.program_id(1), pl.num_programs(1))
    col = pl.multiple_of(tile * FF_DOWN_TILE, FF_DOWN_TILE)
    y_ref[:, pl.ds(col, FF_DOWN_TILE)] = jnp.dot(u_ref[...], w2_ref[...], preferred_element_type=F32)

    @pl.when(pl.program_id(1) == pl.num_programs(1) - 1)
    def _():
        ff = y_ref[...]
        y_ref[...] = x_ref[...] + ff * _rms_scale(ff) * g_ref[...]


def _ffn_down(u, x1, w2, gpost, tm):
    rows = x1.shape[0]
    assert rows % tm == 0
    return pl.pallas_call(
        _ffn_down_kernel,
        grid=(rows // tm, D_MODEL // FF_DOWN_TILE),
        in_specs=[
            pl.BlockSpec((tm, D_FF), lambda i, j: (i, 0)),
            pl.BlockSpec((tm, D_MODEL), lambda i, j: (i, 0)),
            pl.BlockSpec((D_FF, FF_DOWN_TILE), lambda i, j: (0, _serpentine(i, j, D_MODEL // FF_DOWN_TILE))),
            pl.BlockSpec((1, D_MODEL), lambda i, j: (0, 0)),
        ],
        out_specs=pl.BlockSpec((tm, D_MODEL), lambda i, j: (i, 0)),
        out_shape=jax.ShapeDtypeStruct((rows, D_MODEL), F32),
        compiler_params=pltpu.CompilerParams(
            dimension_semantics=("arbitrary", "arbitrary"), vmem_limit_bytes=VMEM_LIMIT_BYTES),
        name="ffn_down",
    )(u, x1, w2, gpost)


def _tile(n, pref):
    return pref if n % pref == 0 else math.gcd(n, pref)


def _trunk(x, meta_proj, w, batch, seq):
    rows = batch * seq
    xf = x.reshape(rows, D_MODEL)
    p, lf = _in_proj(xf, w["g_pre_mix"], w["w_in"], w["act_table"], w["lb_logits"], _tile(rows, 1024))
    p_m, lf_m = meta_proj
    o_f, o_b = _scan(p, lf, p_m, lf_m, batch, seq, _tile(seq, 512))
    att = _attn(p, p_m, w["attn_sink"], batch, seq, _tile(seq, 512))
    x1, h2 = _merge(o_f, o_b, p, att, xf, w["w_proj_hg"], w["w_proj_att"], w["w_out"],
                    w["hg_out_gain"], w["g_post_mix"], w["g_pre_ff"], _tile(rows, 256))
    u = _ffn_up(h2, w["w_ff1"], _tile(rows, 2048))
    y = _ffn_down(u, x1, w["w_ff2"], w["g_post_ff"], _tile(rows, 512))
    return y.reshape(batch, seq, D_MODEL)


def _prepare(meta_tokens, w_in, w_proj_hg, w_proj_att, w_out, w_ff1, w_ff2, g_pre_mix, g_post_mix,
             g_pre_ff, g_post_ff, lb_logits, hg_out_gain, attn_sink):
    w = {
        "w_in": w_in[0].astype(BF16),
        "act_table": jnp.asarray(_activation_table()),
        "w_proj_hg": w_proj_hg[0].astype(BF16),
        "w_proj_att": w_proj_att[0].astype(BF16),
        "w_out": w_out[0].astype(BF16),
        "w_ff1": w_ff1[0].astype(BF16),
        "w_ff2": w_ff2[0].astype(BF16),
        "g_pre_mix": g_pre_mix[0].reshape(1, D_MODEL).astype(F32),
        "g_post_mix": g_post_mix[0].reshape(1, D_MODEL).astype(F32),
        "g_pre_ff": g_pre_ff[0].reshape(1, D_MODEL).astype(F32),
        "g_post_ff": g_post_ff[0].reshape(1, D_MODEL).astype(F32),
        "lb_logits": lb_logits.reshape(2, 2 * HG_KDIM).astype(F32),
        "hg_out_gain": hg_out_gain[0].reshape(1, HG_WIDTH).astype(F32),
        "attn_sink": attn_sink[0].astype(F32),
    }
    meta_proj = _in_proj(meta_tokens.astype(F32), w["g_pre_mix"], w["w_in"], w["act_table"], w["lb_logits"],
                         N_META)
    return w, meta_proj


def kernel(x_prompt, x_sample, meta_tokens, w_in, w_proj_hg, w_proj_att, w_out, w_ff1, w_ff2, g_pre_mix, g_post_mix, g_pre_ff, g_post_ff, lb_logits, hg_out_gain, attn_sink):
    w, meta_proj = _prepare(meta_tokens, w_in, w_proj_hg, w_proj_att, w_out, w_ff1, w_ff2, g_pre_mix,
                            g_post_mix, g_pre_ff, g_post_ff, lb_logits, hg_out_gain, attn_sink)
    y_prompt = _trunk(x_prompt, meta_proj, w, x_prompt.shape[0], x_prompt.shape[1])
    y_sample = _trunk(x_sample, meta_proj, w, x_sample.shape[0], x_sample.shape[1])
    return (y_prompt, y_sample)
```

```python
import functools
import math

import jax
import jax.numpy as jnp
from jax import lax
from jax.experimental import pallas as pl
from jax.experimental.pallas import tpu as pltpu

F32 = jnp.float32
BF16 = jnp.bfloat16

D_MODEL = 2048
N_META = 16
HG_HEADS = 8
HG_DK = 128
HG_DV = 128
HG_KDIM = HG_HEADS * HG_DK
HG_WIDTH = HG_HEADS * HG_DV
HG_CHUNK = 64
N_Q_HEADS = 16
N_KV_HEADS = 4
HEAD_DIM = 64
GROUP = N_Q_HEADS // N_KV_HEADS
ATT_WIDTH = N_Q_HEADS * HEAD_DIM
KV_WIDTH = N_KV_HEADS * HEAD_DIM
WINDOW = 128
ATT_BLOCK = 128
D_FF = 4 * D_MODEL
EPS = 1e-6
LOG2E = math.log2(math.e)
IN_COLS = 3 * HG_KDIM + 2 * HG_WIDTH + ATT_WIDTH + 2 * KV_WIDTH + 2 * D_MODEL

VMEM_LIMIT_BYTES = 56 * 1024 * 1024

IN_TILE = 1024
W_F_TILE = (HG_KDIM + HG_WIDTH) // IN_TILE
N_W_TILES = pl.cdiv(IN_COLS, IN_TILE)
N_P_TILES = N_W_TILES - 2
P_COLS = IN_COLS - 2 * HG_KDIM
LAST_TILE_COLS = P_COLS - (N_P_TILES - 1) * IN_TILE
LF_COLS = 2 * HG_KDIM
PB_Q, PB_I, PB_G, PB_AQ = 0, 1, 2, 3
PB_KV = 4 * IN_TILE // (2 * KV_WIDTH)
P_GATE_START = 4 * IN_TILE + 2 * KV_WIDTH
PB_GATE_TILES = tuple(range(4, N_P_TILES))


def _sigmoid(x):
    return 1.0 / (1.0 + jnp.exp(-x))


def _rms_scale(x):
    return lax.rsqrt(jnp.mean(x * x, axis=-1, keepdims=True) + EPS)


def _row_blocks(tm):
    if tm % (4 * 128) != 0:
        return [slice(0, tm)]
    return [slice(r * tm // 4, (r + 1) * tm // 4) for r in range(4)]


def _serpentine(i, j, n):
    return jnp.where(i % 2 == 0, j, n - 1 - j)


def _in_proj_kernel(x_ref, g_ref, w_ref, lbl_ref, p_ref, lf_ref, h_ref):
    first_step = pl.program_id(1) == 0
    j = _serpentine(pl.program_id(0), pl.program_id(1), N_W_TILES)
    row_blocks = _row_blocks(h_ref.shape[0])

    @pl.when(first_step)
    def _():
        x = x_ref[...]
        h_ref[...] = (x * _rms_scale(x) * g_ref[...]).astype(BF16)

    def tile(cond, out_ref, epilogue, width):
        @pl.when(cond)
        def _():
            for rows in row_blocks:
                acc = jnp.dot(h_ref[rows, :], w_ref[:, :width], preferred_element_type=F32)
                out_ref[rows, :width] = epilogue(acc, width).astype(out_ref.dtype)

    def kv_and_gates(acc, width):
        split = P_GATE_START % IN_TILE
        return jnp.concatenate([acc[:, :split], _sigmoid(acc[:, split:])], axis=1)

    def log_forget(acc, width):
        logits = lbl_ref[...]
        e = jnp.exp(logits - jnp.max(logits, axis=0, keepdims=True))
        lb = e[0:1] / jnp.sum(e, axis=0, keepdims=True)
        return jnp.log(lb + (1.0 - lb) * _sigmoid(acc))

    pure_gates = PB_GATE_TILES[0] + 1
    is_silu = (j == PB_Q) | (j == PB_G)
    scale = jnp.where(j == PB_Q, HG_DK ** -0.5, jnp.where(j == PB_AQ, HEAD_DIM ** -0.5 * LOG2E, 1.0))
    c_silu = jnp.where(is_silu, scale, 0.0)
    c_linear = jnp.where(is_silu, 0.0, scale)
    tile(j < PB_GATE_TILES[0], p_ref, lambda a, w: a * (c_silu * _sigmoid(a) + c_linear), IN_TILE)
    tile(j == PB_GATE_TILES[0], p_ref, kv_and_gates, IN_TILE)
    tile((j >= pure_gates) & (j < N_P_TILES - 1), p_ref, lambda a, w: _sigmoid(a), IN_TILE)
    tile(j == N_P_TILES - 1, p_ref, lambda a, w: _sigmoid(a), LAST_TILE_COLS)
    tile(j >= N_P_TILES, lf_ref, log_forget, IN_TILE)


def _in_proj(x, gain, w_bf16, lb_logits2, tm):
    rows = x.shape[0]
    assert rows % tm == 0

    def w_tile(i, j):
        t = _serpentine(i, j, N_W_TILES)
        return jnp.where(t < W_F_TILE, t, jnp.where(t < N_P_TILES, t + 2, t - N_P_TILES + W_F_TILE))

    def p_tile(i, j):
        return jnp.minimum(_serpentine(i, j, N_W_TILES), N_P_TILES - 1)

    def f_tile(i, j):
        return jnp.maximum(_serpentine(i, j, N_W_TILES) - N_P_TILES, 0)

    return pl.pallas_call(
        _in_proj_kernel,
        grid=(rows // tm, N_W_TILES),
        in_specs=[
            pl.BlockSpec((tm, D_MODEL), lambda i, j: (i, 0)),
            pl.BlockSpec((1, D_MODEL), lambda i, j: (0, 0)),
            pl.BlockSpec((D_MODEL, IN_TILE), lambda i, j: (0, w_tile(i, j))),
            pl.BlockSpec((2, IN_TILE), lambda i, j: (0, f_tile(i, j))),
        ],
        out_specs=[
            pl.BlockSpec((tm, IN_TILE), lambda i, j: (i, p_tile(i, j))),
            pl.BlockSpec((tm, IN_TILE), lambda i, j: (i, f_tile(i, j))),
        ],
        out_shape=[
            jax.ShapeDtypeStruct((rows, P_COLS), BF16),
            jax.ShapeDtypeStruct((rows, LF_COLS), F32),
        ],
        scratch_shapes=[pltpu.VMEM((tm, D_MODEL), BF16)],
        compiler_params=pltpu.CompilerParams(
            dimension_semantics=("arbitrary", "arbitrary"), vmem_limit_bytes=VMEM_LIMIT_BYTES),
        name="in_proj",
    )(x, gain, w_bf16, lb_logits2)


def _tri(n, lower):
    r = lax.broadcasted_iota(jnp.int32, (n, n), 0)
    c = lax.broadcasted_iota(jnp.int32, (n, n), 1)
    return (r >= c) if lower else (r <= c)


def _cumsum_rows(tri_bf16, lf):
    width = lf.shape[1]
    hi = lf.astype(BF16)
    lo = (lf - hi.astype(F32)).astype(BF16)
    r = jnp.dot(tri_bf16, jnp.concatenate([hi, lo], axis=1), preferred_element_type=F32)
    return r[:, :width] + r[:, width:]


_NT = (((1,), (1,)), ((), ()))
_TN = (((0,), (0,)), ((), ()))


def _scan_kernel(qf_ref, vf_ref, lff_ref, qb_ref, vb_ref, lfb_ref, vm_ref, lfm_ref,
                 of_ref, ob_ref, sf_ref, sb_ref, qd_ref, ki_ref, ke_ref, dec_ref, *, nchunks):
    n = pl.program_id(1)
    c_len = HG_CHUNK
    masks = (_tri(c_len, True), _tri(c_len, False))
    tris = tuple(jnp.where(m, 1.0, 0.0).astype(BF16) for m in masks)
    end_rows = (c_len - 1, 0)
    q_refs, v_refs, lf_refs = (qf_ref, qb_ref), (vf_ref, vb_ref), (lff_ref, lfb_ref)
    o_refs, s_refs = (of_ref, ob_ref), (sf_ref, sb_ref)

    @pl.when(n == 0)
    def _():
        sb_ref[...] = jnp.zeros_like(sb_ref)
        tri_m = jnp.where(_tri(N_META, True), 1.0, 0.0).astype(BF16)
        lf = lfm_ref[...]
        cum = _cumsum_rows(tri_m, lf)
        k_end = ((1.0 - jnp.exp(lf)) * jnp.exp(cum[N_META - 1:N_META, :] - cum)).astype(BF16)
        for h in range(HG_HEADS):
            sl = slice(h * HG_DK, (h + 1) * HG_DK)
            sf_ref[h] = lax.dot_general(vm_ref[:, sl], k_end[:, sl], _TN, preferred_element_type=F32)

    for c in range(nchunks):
        rows = slice(c * c_len, (c + 1) * c_len)
        for d in range(2):
            lf = lf_refs[d][rows, :]
            cum = _cumsum_rows(tris[d], lf)
            b_end = cum[end_rows[d]:end_rows[d] + 1, :]
            k_inv = (1.0 - jnp.exp(lf)) * jnp.exp(-cum)
            dec = jnp.exp(b_end)
            qd_ref[d, rows, :] = (q_refs[d][rows, :].astype(F32) * jnp.exp(cum)).astype(BF16)
            ki_ref[d, rows, :] = k_inv.astype(BF16)
            ke_ref[d, rows, :] = (k_inv * dec).astype(BF16)
            dec_ref[d, c] = dec

    def body(c, carry):
        chunk = (c, nchunks - 1 - c)
        work = []
        for d in range(2):
            rows = pl.ds(pl.multiple_of(chunk[d] * c_len, c_len), c_len)
            dec = dec_ref[d, chunk[d]]
            for h in range(HG_HEADS):
                sl = slice(h * HG_DK, (h + 1) * HG_DK)
                qd = qd_ref[d, rows, sl]
                v = v_refs[d][rows, sl]
                st = s_refs[d][h]
                scores = lax.dot_general(qd, ki_ref[d, rows, sl], _NT, preferred_element_type=F32)
                inter = lax.dot_general(qd, st.astype(BF16), _NT, preferred_element_type=F32)
                inc = lax.dot_general(v, ke_ref[d, rows, sl], _TN, preferred_element_type=F32)
                s_refs[d][h] = st * dec[:, sl] + inc
                work.append((d, rows, sl, scores, inter, v))
        for d, rows, sl, scores, inter, v in work:
            intra = jnp.dot(jnp.where(masks[d], scores, 0.0).astype(BF16), v, preferred_element_type=F32)
            o_refs[d][rows, sl] = intra + inter
        return carry

    lax.fori_loop(0, nchunks, body, 0, unroll=2 if nchunks % 2 == 0 else 1)


def _scan(p, lf, p_meta, lf_meta, batch, seq, rb):
    rows = batch * seq
    assert seq % rb == 0 and rb % HG_CHUNK == 0
    nb = seq // rb

    def fwd(col):
        return lambda b, n: (b * nb + n, col)

    def bwd(col):
        return lambda b, n: (b * nb + nb - 1 - n, col)

    return pl.pallas_call(
        functools.partial(_scan_kernel, nchunks=rb // HG_CHUNK),
        grid=(batch, nb),
        in_specs=[
            pl.BlockSpec((rb, HG_KDIM), fwd(PB_Q)),
            pl.BlockSpec((rb, HG_WIDTH), fwd(PB_I)),
            pl.BlockSpec((rb, HG_KDIM), fwd(0)),
            pl.BlockSpec((rb, HG_KDIM), bwd(PB_Q)),
            pl.BlockSpec((rb, HG_WIDTH), bwd(PB_I)),
            pl.BlockSpec((rb, HG_KDIM), bwd(1)),
            pl.BlockSpec((N_META, HG_WIDTH), lambda b, n: (0, PB_I)),
            pl.BlockSpec((N_META, HG_KDIM), lambda b, n: (0, 0)),
        ],
        out_specs=[
            pl.BlockSpec((rb, HG_WIDTH), fwd(0)),
            pl.BlockSpec((rb, HG_WIDTH), bwd(0)),
        ],
        out_shape=[
            jax.ShapeDtypeStruct((rows, HG_WIDTH), F32),
            jax.ShapeDtypeStruct((rows, HG_WIDTH), F32),
        ],
        scratch_shapes=[
            pltpu.VMEM((HG_HEADS, HG_DV, HG_DK), F32),
            pltpu.VMEM((HG_HEADS, HG_DV, HG_DK), F32),
            pltpu.VMEM((2, rb, HG_KDIM), BF16),
            pltpu.VMEM((2, rb, HG_KDIM), BF16),
            pltpu.VMEM((2, rb, HG_KDIM), BF16),
            pltpu.VMEM((2, rb // HG_CHUNK, 1, HG_KDIM), F32),
        ],
        compiler_params=pltpu.CompilerParams(
            dimension_semantics=("arbitrary", "arbitrary"), vmem_limit_bytes=VMEM_LIMIT_BYTES),
        name="hgrn2_scan",
    )(p, p, lf, p, p, lf, p_meta, lf_meta)


N_KEYS = 3 * ATT_BLOCK + N_META


def _alibi_slope(head):
    return 2.0 ** (-8.0 * (head + 1) / N_Q_HEADS)


LANES = 128
HEADS_PER_GROUP = LANES // HEAD_DIM


def _attn_kernel(sink_ref, q_ref, kvc_ref, kvp_ref, kvn_ref, kvm_ref, o_ref, bias_ref, *, seq, qb):
    n = pl.program_id(1)
    nsub = qb // ATT_BLOCK
    neg_inf = -jnp.inf

    @pl.when((pl.program_id(0) == 0) & (n == 0))
    def _():
        u = lax.broadcasted_iota(jnp.int32, (N_KEYS, ATT_BLOCK), 0)
        r = lax.broadcasted_iota(jnp.int32, (N_KEYS, ATT_BLOCK), 1)
        dist = jnp.abs(u - ATT_BLOCK - r)
        is_meta = u >= 3 * ATT_BLOCK
        dist_f = dist.astype(F32)
        for head in range(N_Q_HEADS):
            inside = jnp.where(dist <= WINDOW, (-_alibi_slope(head) * LOG2E) * dist_f, neg_inf)
            bias_ref[head] = jnp.where(is_meta, 0.0, inside)

    lane = lax.broadcasted_iota(jnp.int32, (N_KEYS, LANES), 1)
    low_half = lane < HEAD_DIM
    for i in range(nsub):
        rows = slice(i * ATT_BLOCK, (i + 1) * ATT_BLOCK)
        kv_prev = kvp_ref[...] if i == 0 else kvc_ref[(i - 1) * ATT_BLOCK:i * ATT_BLOCK, :]
        kv_next = kvn_ref[...] if i == nsub - 1 else kvc_ref[(i + 1) * ATT_BLOCK:(i + 2) * ATT_BLOCK, :]
        kv_all = jnp.concatenate([kv_prev, kvc_ref[rows, :], kv_next, kvm_ref[...]], axis=0)
        pen_prev = jnp.where(n * qb + (i - 1) * ATT_BLOCK < 0, neg_inf, 0.0) if i == 0 else None
        pen_next = jnp.where(n * qb + (i + 1) * ATT_BLOCK >= seq, neg_inf, 0.0) if i == nsub - 1 else None
        for m in range(N_KV_HEADS // HEADS_PER_GROUP):
            kcol = kv_all[:, m * LANES:(m + 1) * LANES]
            vcol = kv_all[:, KV_WIDTH + m * LANES:KV_WIDTH + (m + 1) * LANES]
            kswap = jnp.concatenate([kcol[:, HEAD_DIM:], kcol[:, :HEAD_DIM]], axis=1)
            probs, inv_denoms = [], []
            for e in range(HEADS_PER_GROUP):
                h = m * HEADS_PER_GROUP + e
                k_lo = jnp.where(low_half, kcol if e == 0 else kswap, 0).astype(BF16)
                k_hi = jnp.where(low_half, 0, kswap if e == 0 else kcol).astype(BF16)
                qcols = [q_ref[rows, (2 * h + c) * LANES:(2 * h + c + 1) * LANES] for c in range(2)]
                s_all = lax.dot_general(jnp.concatenate([k_lo, k_hi], axis=0), jnp.concatenate(qcols, axis=0),
                                        _NT, preferred_element_type=F32)
                for g in range(GROUP):
                    head = h * GROUP + g
                    half, c = g % 2, g // 2
                    s = s_all[half * N_KEYS:(half + 1) * N_KEYS, c * ATT_BLOCK:(c + 1) * ATT_BLOCK]
                    bias = bias_ref[head]
                    if pen_prev is not None or pen_next is not None:
                        pieces = [bias[0:ATT_BLOCK], bias[ATT_BLOCK:2 * ATT_BLOCK],
                                  bias[2 * ATT_BLOCK:3 * ATT_BLOCK], bias[3 * ATT_BLOCK:]]
                        if pen_prev is not None:
                            pieces[0] = pieces[0] + pen_prev
                        if pen_next is not None:
                            pieces[2] = pieces[2] + pen_next
                        bias = jnp.concatenate(pieces, axis=0)
                    s = s + bias
                    sink = sink_ref[head] * LOG2E
                    mx = jnp.maximum(jnp.max(s, axis=0, keepdims=True), sink)
                    p = jnp.exp2(s - mx)
                    denom = jnp.sum(p, axis=0, keepdims=True) + jnp.exp2(sink - mx)
                    probs.append(p.astype(BF16))
                    inv_denoms.append(1.0 / denom)
            o_t = lax.dot_general(vcol, jnp.concatenate(probs, axis=1), _TN, preferred_element_type=F32)
            for e in range(HEADS_PER_GROUP):
                h = m * HEADS_PER_GROUP + e
                for c in range(2):
                    pair = []
                    for half in range(2):
                        idx = e * GROUP + 2 * c + half
                        pair.append(o_t[e * HEAD_DIM:(e + 1) * HEAD_DIM, idx * ATT_BLOCK:(idx + 1) * ATT_BLOCK]
                                    * inv_denoms[idx])
                    out = jnp.concatenate(pair, axis=0).T
                    o_ref[rows, (2 * h + c) * LANES:(2 * h + c + 1) * LANES] = out.astype(BF16)


def _attn(p, p_meta, sink, batch, seq, qb):
    rows = batch * seq
    assert seq % qb == 0 and qb % ATT_BLOCK == 0
    nb = seq // qb
    per = qb // ATT_BLOCK
    last = rows // ATT_BLOCK - 1
    q_col, kv_col = PB_AQ, PB_KV

    return pl.pallas_call(
        functools.partial(_attn_kernel, seq=seq, qb=qb),
        grid=(batch, nb),
        in_specs=[
            pl.BlockSpec(memory_space=pltpu.SMEM),
            pl.BlockSpec((qb, ATT_WIDTH), lambda b, n: (b * nb + n, q_col)),
            pl.BlockSpec((qb, 2 * KV_WIDTH), lambda b, n: (b * nb + n, kv_col)),
            pl.BlockSpec((ATT_BLOCK, 2 * KV_WIDTH),
                         lambda b, n: (jnp.maximum((b * nb + n) * per - 1, 0), kv_col)),
            pl.BlockSpec((ATT_BLOCK, 2 * KV_WIDTH),
                         lambda b, n: (jnp.minimum((b * nb + n + 1) * per, last), kv_col)),
            pl.BlockSpec((N_META, 2 * KV_WIDTH), lambda b, n: (0, kv_col)),
        ],
        out_specs=pl.BlockSpec((qb, ATT_WIDTH), lambda b, n: (b * nb + n, 0)),
        out_shape=jax.ShapeDtypeStruct((rows, ATT_WIDTH), BF16),
        scratch_shapes=[pltpu.VMEM((N_Q_HEADS, N_KEYS, ATT_BLOCK), F32)],
        compiler_params=pltpu.CompilerParams(
            dimension_semantics=("arbitrary", "arbitrary"), vmem_limit_bytes=VMEM_LIMIT_BYTES),
        name="window_attn",
    )(sink, p, p, p, p, p_meta)


def _merge_kernel(of_ref, ob_ref, g_ref, att_ref, gt0_ref, gt1_ref, gt2_ref, gt3_ref, gt4_ref, x_ref,
                  wa_ref, wb_ref, wo_ref, hgain_ref, gpost_ref, gpre_ref, x1_ref, h2_ref, *, nsplit):
    sub = x_ref.shape[0] // nsplit
    half = P_GATE_START % IN_TILE
    for r in range(nsplit):
        rows = slice(r * sub, (r + 1) * sub)
        gate_a = jnp.concatenate([gt0_ref[rows, half:], gt1_ref[rows, :], gt2_ref[rows, :half]], axis=1)
        gate_b = jnp.concatenate([gt2_ref[rows, half:], gt3_ref[rows, :], gt4_ref[rows, :half]], axis=1)
        y = gate_b.astype(F32) * jnp.dot(att_ref[rows, :], wb_ref[...], preferred_element_type=F32)
        o = of_ref[rows, :] + ob_ref[rows, :]
        parts = []
        for h in range(HG_HEADS):
            oh = o[:, h * HG_DV:(h + 1) * HG_DV]
            parts.append(oh * _rms_scale(oh))
        a = jnp.concatenate(parts, axis=1) * hgain_ref[...] * g_ref[rows, :].astype(F32)
        y = y + gate_a.astype(F32) * jnp.dot(a.astype(BF16), wa_ref[...], preferred_element_type=F32)
        z = jnp.dot(y.astype(BF16), wo_ref[...], preferred_element_type=F32)
        x1 = x_ref[rows, :] + z * _rms_scale(z) * gpost_ref[...]
        x1_ref[rows, :] = x1
        h2_ref[rows, :] = (x1 * _rms_scale(x1) * gpre_ref[...]).astype(BF16)


def _merge(o_f, o_b, p, att, x, wa, wb, wo, hgain, gpost, gpre, tm):
    rows = x.shape[0]
    assert rows % tm == 0
    row = lambda c: (lambda i: (i, c))
    const = lambda i: (0, 0)
    single = pl.Buffered(1)
    nsplit = 2 if tm % 256 == 0 else 1
    return pl.pallas_call(
        functools.partial(_merge_kernel, nsplit=nsplit),
        grid=(rows // tm,),
        in_specs=[
            pl.BlockSpec((tm, HG_WIDTH), row(0)),
            pl.BlockSpec((tm, HG_WIDTH), row(0)),
            pl.BlockSpec((tm, HG_WIDTH), row(PB_G)),
            pl.BlockSpec((tm, ATT_WIDTH), row(0)),
            *[pl.BlockSpec((tm, IN_TILE), row(t)) for t in PB_GATE_TILES],
            pl.BlockSpec((tm, D_MODEL), row(0)),
            pl.BlockSpec((HG_WIDTH, D_MODEL), const, pipeline_mode=single),
            pl.BlockSpec((ATT_WIDTH, D_MODEL), const, pipeline_mode=single),
            pl.BlockSpec((D_MODEL, D_MODEL), const, pipeline_mode=single),
            pl.BlockSpec((1, HG_WIDTH), const),
            pl.BlockSpec((1, D_MODEL), const),
            pl.BlockSpec((1, D_MODEL), const),
        ],
        out_specs=[
            pl.BlockSpec((tm, D_MODEL), row(0)),
            pl.BlockSpec((tm, D_MODEL), row(0)),
        ],
        out_shape=[
            jax.ShapeDtypeStruct((rows, D_MODEL), F32),
            jax.ShapeDtypeStruct((rows, D_MODEL), BF16),
        ],
        compiler_params=pltpu.CompilerParams(
            dimension_semantics=("arbitrary",), vmem_limit_bytes=VMEM_LIMIT_BYTES),
        name="merge_out",
    )(o_f, o_b, p, att, *([p] * len(PB_GATE_TILES)), x, wa, wb, wo, hgain, gpost, gpre)


FF_UP_TILE = 1024
FF_DOWN_TILE = 512


def _ffn_up_kernel(h_ref, w1_ref, u_ref, *, nsplit):
    sub = h_ref.shape[0] // nsplit
    for r in range(nsplit):
        rows = slice(r * sub, (r + 1) * sub)
        u = jnp.maximum(jnp.dot(h_ref[rows, :], w1_ref[...], preferred_element_type=F32), 0.0)
        u_ref[rows, :] = (u * u).astype(BF16)


def _ffn_up(h2, w1, tm):
    rows = h2.shape[0]
    assert rows % tm == 0
    nsplit = 4 if tm % (4 * 128) == 0 else 1
    return pl.pallas_call(
        functools.partial(_ffn_up_kernel, nsplit=nsplit),
        grid=(rows // tm, D_FF // FF_UP_TILE),
        in_specs=[
            pl.BlockSpec((tm, D_MODEL), lambda i, j: (i, 0)),
            pl.BlockSpec((D_MODEL, FF_UP_TILE), lambda i, j: (0, j)),
        ],
        out_specs=pl.BlockSpec((tm, FF_UP_TILE), lambda i, j: (i, j)),
        out_shape=jax.ShapeDtypeStruct((rows, D_FF), BF16),
        compiler_params=pltpu.CompilerParams(
            dimension_semantics=("arbitrary", "arbitrary"), vmem_limit_bytes=VMEM_LIMIT_BYTES),
        name="ffn_up",
    )(h2, w1)


def _ffn_down_kernel(u_ref, x_ref, w2_ref, g_ref, y_ref):
    tile = _serpentine(pl.program_id(0), pl.program_id(1), pl.num_programs(1))
    cols = pl.ds(pl.multiple_of(tile * FF_DOWN_TILE, FF_DOWN_TILE), FF_DOWN_TILE)
    last = pl.program_id(1) == pl.num_programs(1) - 1

    @pl.when(jnp.logical_not(last))
    def _():
        y_ref[:, cols] = jnp.dot(u_ref[...], w2_ref[...], preferred_element_type=F32)

    @pl.when(last)
    def _():
        half = y_ref.shape[0] // 2
        for r in range(2):
            rows = slice(r * half, (r + 1) * half)
            y_ref[rows, cols] = jnp.dot(u_ref[rows, :], w2_ref[...], preferred_element_type=F32)
            ff = y_ref[rows, :]
            y_ref[rows, :] = x_ref[rows, :] + ff * _rms_scale(ff) * g_ref[...]


def _ffn_down(u, x1, w2, gpost, tm):
    rows = x1.shape[0]
    assert rows % tm == 0
    return pl.pallas_call(
        _ffn_down_kernel,
        grid=(rows // tm, D_MODEL // FF_DOWN_TILE),
        in_specs=[
            pl.BlockSpec((tm, D_FF), lambda i, j: (i, 0)),
            pl.BlockSpec((tm, D_MODEL), lambda i, j: (i, 0)),
            pl.BlockSpec((D_FF, FF_DOWN_TILE), lambda i, j: (0, _serpentine(i, j, D_MODEL // FF_DOWN_TILE))),
            pl.BlockSpec((1, D_MODEL), lambda i, j: (0, 0)),
        ],
        out_specs=pl.BlockSpec((tm, D_MODEL), lambda i, j: (i, 0)),
        out_shape=jax.ShapeDtypeStruct((rows, D_MODEL), F32),
        compiler_params=pltpu.CompilerParams(
            dimension_semantics=("arbitrary", "arbitrary"), vmem_limit_bytes=VMEM_LIMIT_BYTES),
        name="ffn_down",
    )(u, x1, w2, gpost)


def _tile(n, pref):
    return pref if n % pref == 0 else math.gcd(n, pref)


def _trunk(x, meta_proj, w, batch, seq):
    rows = batch * seq
    xf = x.reshape(rows, D_MODEL)
    p, lf = _in_proj(xf, w["g_pre_mix"], w["w_in"], w["lb_logits"], _tile(rows, 1024))
    p_m, lf_m = meta_proj
    o_f, o_b = _scan(p, lf, p_m, lf_m, batch, seq, _tile(seq, 512))
    att = _attn(p, p_m, w["attn_sink"], batch, seq, _tile(seq, 1024))
    x1, h2 = _merge(o_f, o_b, p, att, xf, w["w_proj_hg"], w["w_proj_att"], w["w_out"],
                    w["hg_out_gain"], w["g_post_mix"], w["g_pre_ff"], _tile(rows, 256))
    u = _ffn_up(h2, w["w_ff1"], _tile(rows, 2048))
    y = _ffn_down(u, x1, w["w_ff2"], w["g_post_ff"], _tile(rows, 512))
    return y.reshape(batch, seq, D_MODEL)


def _prepare(meta_tokens, w_in, w_proj_hg, w_proj_att, w_out, w_ff1, w_ff2, g_pre_mix, g_post_mix,
             g_pre_ff, g_post_ff, lb_logits, hg_out_gain, attn_sink):
    w = {
        "w_in": w_in[0].astype(BF16),
        "w_proj_hg": w_proj_hg[0].astype(BF16),
        "w_proj_att": w_proj_att[0].astype(BF16),
        "w_out": w_out[0].astype(BF16),
        "w_ff1": w_ff1[0].astype(BF16),
        "w_ff2": w_ff2[0].astype(BF16),
        "g_pre_mix": g_pre_mix[0].reshape(1, D_MODEL).astype(F32),
        "g_post_mix": g_post_mix[0].reshape(1, D_MODEL).astype(F32),
        "g_pre_ff": g_pre_ff[0].reshape(1, D_MODEL).astype(F32),
        "g_post_ff": g_post_ff[0].reshape(1, D_MODEL).astype(F32),
        "lb_logits": lb_logits.reshape(2, 2 * HG_KDIM).astype(F32),
        "hg_out_gain": hg_out_gain[0].reshape(1, HG_WIDTH).astype(F32),
        "attn_sink": attn_sink[0].astype(F32),
    }
    meta_proj = _in_proj(meta_tokens.astype(F32), w["g_pre_mix"], w["w_in"], w["lb_logits"], N_META)
    return w, meta_proj


def kernel(x_prompt, x_sample, meta_tokens, w_in, w_proj_hg, w_proj_att, w_out, w_ff1, w_ff2, g_pre_mix, g_post_mix, g_pre_ff, g_post_ff, lb_logits, hg_out_gain, attn_sink):
    w, meta_proj = _prepare(meta_tokens, w_in, w_proj_hg, w_proj_att, w_out, w_ff1, w_ff2, g_pre_mix,
                            g_post_mix, g_pre_ff, g_post_ff, lb_logits, hg_out_gain, attn_sink)
    y_prompt = _trunk(x_prompt, meta_proj, w, x_prompt.shape[0], x_prompt.shape[1])
    y_sample = _trunk(x_sample, meta_proj, w, x_sample.shape[0], x_sample.shape[1])
    return (y_prompt, y_sample)
```

```python
import functools
import math

import jax
import jax.numpy as jnp
from jax import lax
from jax.experimental import pallas as pl
from jax.experimental.pallas import tpu as pltpu

F32 = jnp.float32
BF16 = jnp.bfloat16

D_MODEL = 2048
N_META = 16
HG_HEADS = 8
HG_DK = 128
HG_DV = 128
HG_KDIM = HG_HEADS * HG_DK
HG_WIDTH = HG_HEADS * HG_DV
HG_CHUNK = 64
N_Q_HEADS = 16
N_KV_HEADS = 4
HEAD_DIM = 64
GROUP = N_Q_HEADS // N_KV_HEADS
ATT_WIDTH = N_Q_HEADS * HEAD_DIM
KV_WIDTH = N_KV_HEADS * HEAD_DIM
WINDOW = 128
ATT_BLOCK = 128
D_FF = 4 * D_MODEL
EPS = 1e-6
LOG2E = math.log2(math.e)
IN_COLS = 3 * HG_KDIM + 2 * HG_WIDTH + ATT_WIDTH + 2 * KV_WIDTH + 2 * D_MODEL

VMEM_LIMIT_BYTES = 56 * 1024 * 1024

IN_TILE = 1024
W_F_TILE = (HG_KDIM + HG_WIDTH) // IN_TILE
N_W_TILES = pl.cdiv(IN_COLS, IN_TILE)
N_P_TILES = N_W_TILES - 2
P_COLS = IN_COLS - 2 * HG_KDIM
LAST_TILE_COLS = P_COLS - (N_P_TILES - 1) * IN_TILE
LF_COLS = 2 * HG_KDIM
PB_Q, PB_I, PB_G, PB_AQ = 0, 1, 2, 3
PB_KV = 4 * IN_TILE // (2 * KV_WIDTH)
P_GATE_START = 4 * IN_TILE + 2 * KV_WIDTH
PB_GATE_TILES = tuple(range(4, N_P_TILES))


def _sigmoid(x):
    return 1.0 / (1.0 + jnp.exp(-x))


def _rms_scale(x):
    return lax.rsqrt(jnp.mean(x * x, axis=-1, keepdims=True) + EPS)


def _row_blocks(tm):
    if tm % (8 * 128) != 0:
        return [slice(0, tm)]
    return [slice(r * tm // 8, (r + 1) * tm // 8) for r in range(8)]


def _serpentine(i, j, n):
    return jnp.where(i % 2 == 0, j, n - 1 - j)


def _in_proj_kernel(x_ref, g_ref, w_ref, lbl_ref, p_ref, lf_ref, h_ref):
    first_step = pl.program_id(1) == 0
    j = _serpentine(pl.program_id(0), pl.program_id(1), N_W_TILES)
    row_blocks = _row_blocks(h_ref.shape[0])

    @pl.when(first_step)
    def _():
        x = x_ref[...]
        h_ref[...] = (x * _rms_scale(x) * g_ref[...]).astype(BF16)

    def tile(cond, out_ref, epilogue, width):
        @pl.when(cond)
        def _():
            w = w_ref[:, :width].astype(BF16)
            for rows in row_blocks:
                acc = jnp.dot(h_ref[rows, :], w, preferred_element_type=F32)
                out_ref[rows, :width] = epilogue(acc, width).astype(out_ref.dtype)

    def kv_and_gates(acc, width):
        split = P_GATE_START % IN_TILE
        return jnp.concatenate([acc[:, :split], _sigmoid(acc[:, split:])], axis=1)

    def log_forget(acc, width):
        logits = lbl_ref[...]
        e = jnp.exp(logits - jnp.max(logits, axis=0, keepdims=True))
        lb = e[0:1] / jnp.sum(e, axis=0, keepdims=True)
        return jnp.log(lb + (1.0 - lb) * _sigmoid(acc))

    pure_gates = PB_GATE_TILES[0] + 1
    is_silu = (j == PB_Q) | (j == PB_G)
    scale = jnp.where(j == PB_Q, HG_DK ** -0.5, jnp.where(j == PB_AQ, HEAD_DIM ** -0.5 * LOG2E, 1.0))
    c_silu = jnp.where(is_silu, scale, 0.0)
    c_linear = jnp.where(is_silu, 0.0, scale)
    tile(j < PB_GATE_TILES[0], p_ref, lambda a, w: a * (c_silu * _sigmoid(a) + c_linear), IN_TILE)
    tile(j == PB_GATE_TILES[0], p_ref, kv_and_gates, IN_TILE)
    tile((j >= pure_gates) & (j < N_P_TILES - 1), p_ref, lambda a, w: _sigmoid(a), IN_TILE)
    tile(j == N_P_TILES - 1, p_ref, lambda a, w: _sigmoid(a), LAST_TILE_COLS)
    tile(j >= N_P_TILES, lf_ref, log_forget, IN_TILE)


def _in_proj(x, gain, w_bf16, lb_logits2, tm):
    rows = x.shape[0]
    assert rows % tm == 0

    def w_tile(i, j):
        t = _serpentine(i, j, N_W_TILES)
        return jnp.where(t < W_F_TILE, t, jnp.where(t < N_P_TILES, t + 2, t - N_P_TILES + W_F_TILE))

    def p_tile(i, j):
        return jnp.minimum(_serpentine(i, j, N_W_TILES), N_P_TILES - 1)

    def f_tile(i, j):
        return jnp.maximum(_serpentine(i, j, N_W_TILES) - N_P_TILES, 0)

    return pl.pallas_call(
        _in_proj_kernel,
        grid=(rows // tm, N_W_TILES),
        in_specs=[
            pl.BlockSpec((tm, D_MODEL), lambda i, j: (i, 0)),
            pl.BlockSpec((1, D_MODEL), lambda i, j: (0, 0)),
            pl.BlockSpec((D_MODEL, IN_TILE), lambda i, j: (0, w_tile(i, j))),
            pl.BlockSpec((2, IN_TILE), lambda i, j: (0, f_tile(i, j))),
        ],
        out_specs=[
            pl.BlockSpec((tm, IN_TILE), lambda i, j: (i, p_tile(i, j))),
            pl.BlockSpec((tm, IN_TILE), lambda i, j: (i, f_tile(i, j))),
        ],
        out_shape=[
            jax.ShapeDtypeStruct((rows, P_COLS), BF16),
            jax.ShapeDtypeStruct((rows, LF_COLS), F32),
        ],
        scratch_shapes=[pltpu.VMEM((tm, D_MODEL), BF16)],
        compiler_params=pltpu.CompilerParams(
            dimension_semantics=("arbitrary", "arbitrary"), vmem_limit_bytes=VMEM_LIMIT_BYTES),
        name="in_proj",
    )(x, gain, w_bf16, lb_logits2)


def _tri(n, lower):
    r = lax.broadcasted_iota(jnp.int32, (n, n), 0)
    c = lax.broadcasted_iota(jnp.int32, (n, n), 1)
    return (r >= c) if lower else (r <= c)


def _cumsum_rows(tri_bf16, lf):
    width = lf.shape[1]
    hi = lf.astype(BF16)
    lo = (lf - hi.astype(F32)).astype(BF16)
    r = jnp.dot(tri_bf16, jnp.concatenate([hi, lo], axis=1), preferred_element_type=F32)
    return r[:, :width] + r[:, width:]


_NT = (((1,), (1,)), ((), ()))
_TN = (((0,), (0,)), ((), ()))


def _scan_kernel(qf_ref, vf_ref, lff_ref, qb_ref, vb_ref, lfb_ref, vm_ref, lfm_ref,
                 of_ref, ob_ref, sf_ref, sb_ref, qd_ref, ki_ref, ke_ref, dec_ref, *, nchunks):
    n = pl.program_id(1)
    c_len = HG_CHUNK
    masks = (_tri(c_len, True), _tri(c_len, False))
    tris = tuple(jnp.where(m, 1.0, 0.0).astype(BF16) for m in masks)
    end_rows = (c_len - 1, 0)
    q_refs, v_refs, lf_refs = (qf_ref, qb_ref), (vf_ref, vb_ref), (lff_ref, lfb_ref)
    o_refs, s_refs = (of_ref, ob_ref), (sf_ref, sb_ref)

    @pl.when(n == 0)
    def _():
        sb_ref[...] = jnp.zeros_like(sb_ref)
        tri_m = jnp.where(_tri(N_META, True), 1.0, 0.0).astype(BF16)
        lf = lfm_ref[...]
        cum = _cumsum_rows(tri_m, lf)
        k_end = ((1.0 - jnp.exp(lf)) * jnp.exp(cum[N_META - 1:N_META, :] - cum)).astype(BF16)
        for h in range(HG_HEADS):
            sl = slice(h * HG_DK, (h + 1) * HG_DK)
            sf_ref[h] = lax.dot_general(vm_ref[:, sl], k_end[:, sl], _TN, preferred_element_type=F32)

    for c in range(nchunks):
        rows = slice(c * c_len, (c + 1) * c_len)
        for d in range(2):
            lf = lf_refs[d][rows, :]
            cum = _cumsum_rows(tris[d], lf)
            b_end = cum[end_rows[d]:end_rows[d] + 1, :]
            k_inv = (1.0 - jnp.exp(lf)) * jnp.exp(-cum)
            dec = jnp.exp(b_end)
            qd_ref[d, rows, :] = (q_refs[d][rows, :].astype(F32) * jnp.exp(cum)).astype(BF16)
            ki_ref[d, rows, :] = k_inv.astype(BF16)
            ke_ref[d, rows, :] = (k_inv * dec).astype(BF16)
            dec_ref[d, c] = dec

    def body(c, carry):
        chunk = (c, nchunks - 1 - c)
        work = []
        for d in range(2):
            rows = pl.ds(pl.multiple_of(chunk[d] * c_len, c_len), c_len)
            dec = dec_ref[d, chunk[d]]
            for h in range(HG_HEADS):
                sl = slice(h * HG_DK, (h + 1) * HG_DK)
                qd = qd_ref[d, rows, sl]
                v = v_refs[d][rows, sl]
                st = s_refs[d][h]
                scores = lax.dot_general(qd, ki_ref[d, rows, sl], _NT, preferred_element_type=F32)
                inter = lax.dot_general(qd, st.astype(BF16), _NT, preferred_element_type=F32)
                inc = lax.dot_general(v, ke_ref[d, rows, sl], _TN, preferred_element_type=F32)
                s_refs[d][h] = st * dec[:, sl] + inc
                work.append((d, rows, sl, scores, inter, v))
        for d, rows, sl, scores, inter, v in work:
            intra = jnp.dot(jnp.where(masks[d], scores, 0.0).astype(BF16), v, preferred_element_type=F32)
            o_refs[d][rows, sl] = intra + inter
        return carry

    lax.fori_loop(0, nchunks, body, 0, unroll=2 if nchunks % 2 == 0 else 1)


def _scan(p, lf, p_meta, lf_meta, batch, seq, rb):
    rows = batch * seq
    assert seq % rb == 0 and rb % HG_CHUNK == 0
    nb = seq // rb

    def fwd(col):
        return lambda b, n: (b * nb + n, col)

    def bwd(col):
        return lambda b, n: (b * nb + nb - 1 - n, col)

    return pl.pallas_call(
        functools.partial(_scan_kernel, nchunks=rb // HG_CHUNK),
        grid=(batch, nb),
        in_specs=[
            pl.BlockSpec((rb, HG_KDIM), fwd(PB_Q)),
            pl.BlockSpec((rb, HG_WIDTH), fwd(PB_I)),
            pl.BlockSpec((rb, HG_KDIM), fwd(0)),
            pl.BlockSpec((rb, HG_KDIM), bwd(PB_Q)),
            pl.BlockSpec((rb, HG_WIDTH), bwd(PB_I)),
            pl.BlockSpec((rb, HG_KDIM), bwd(1)),
            pl.BlockSpec((N_META, HG_WIDTH), lambda b, n: (0, PB_I)),
            pl.BlockSpec((N_META, HG_KDIM), lambda b, n: (0, 0)),
        ],
        out_specs=[
            pl.BlockSpec((rb, HG_WIDTH), fwd(0)),
            pl.BlockSpec((rb, HG_WIDTH), bwd(0)),
        ],
        out_shape=[
            jax.ShapeDtypeStruct((rows, HG_WIDTH), F32),
            jax.ShapeDtypeStruct((rows, HG_WIDTH), F32),
        ],
        scratch_shapes=[
            pltpu.VMEM((HG_HEADS, HG_DV, HG_DK), F32),
            pltpu.VMEM((HG_HEADS, HG_DV, HG_DK), F32),
            pltpu.VMEM((2, rb, HG_KDIM), BF16),
            pltpu.VMEM((2, rb, HG_KDIM), BF16),
            pltpu.VMEM((2, rb, HG_KDIM), BF16),
            pltpu.VMEM((2, rb // HG_CHUNK, 1, HG_KDIM), F32),
        ],
        compiler_params=pltpu.CompilerParams(
            dimension_semantics=("arbitrary", "arbitrary"), vmem_limit_bytes=VMEM_LIMIT_BYTES),
        name="hgrn2_scan",
    )(p, p, lf, p, p, lf, p_meta, lf_meta)


N_KEYS = 3 * ATT_BLOCK + N_META


def _alibi_slope(head):
    return 2.0 ** (-8.0 * (head + 1) / N_Q_HEADS)


LANES = 128
HEADS_PER_GROUP = LANES // HEAD_DIM


def _attn_kernel(sink_ref, q_ref, kvc_ref, kvp_ref, kvn_ref, kvm_ref, o_ref, bias_ref, *, seq, qb):
    n = pl.program_id(1)
    nsub = qb // ATT_BLOCK
    neg_inf = -jnp.inf

    @pl.when((pl.program_id(0) == 0) & (n == 0))
    def _():
        u = lax.broadcasted_iota(jnp.int32, (N_KEYS, ATT_BLOCK), 0)
        r = lax.broadcasted_iota(jnp.int32, (N_KEYS, ATT_BLOCK), 1)
        dist = jnp.abs(u - ATT_BLOCK - r)
        is_meta = u >= 3 * ATT_BLOCK
        dist_f = dist.astype(F32)
        for head in range(N_Q_HEADS):
            inside = jnp.where(dist <= WINDOW, (-_alibi_slope(head) * LOG2E) * dist_f, neg_inf)
            bias_ref[head] = jnp.where(is_meta, 0.0, inside)

    lane = lax.broadcasted_iota(jnp.int32, (N_KEYS, LANES), 1)
    low_half = lane < HEAD_DIM
    for i in range(nsub):
        rows = slice(i * ATT_BLOCK, (i + 1) * ATT_BLOCK)
        kv_prev = kvp_ref[...] if i == 0 else kvc_ref[(i - 1) * ATT_BLOCK:i * ATT_BLOCK, :]
        kv_next = kvn_ref[...] if i == nsub - 1 else kvc_ref[(i + 1) * ATT_BLOCK:(i + 2) * ATT_BLOCK, :]
        kv_all = jnp.concatenate([kv_prev, kvc_ref[rows, :], kv_next, kvm_ref[...]], axis=0)
        pen_prev = jnp.where(n * qb + (i - 1) * ATT_BLOCK < 0, neg_inf, 0.0) if i == 0 else None
        pen_next = jnp.where(n * qb + (i + 1) * ATT_BLOCK >= seq, neg_inf, 0.0) if i == nsub - 1 else None
        for m in range(N_KV_HEADS // HEADS_PER_GROUP):
            kcol = kv_all[:, m * LANES:(m + 1) * LANES]
            vcol = kv_all[:, KV_WIDTH + m * LANES:KV_WIDTH + (m + 1) * LANES]
            kswap = jnp.concatenate([kcol[:, HEAD_DIM:], kcol[:, :HEAD_DIM]], axis=1)
            probs, inv_denoms = [], []
            for e in range(HEADS_PER_GROUP):
                h = m * HEADS_PER_GROUP + e
                k_lo = jnp.where(low_half, kcol if e == 0 else kswap, 0).astype(BF16)
                k_hi = jnp.where(low_half, 0, kswap if e == 0 else kcol).astype(BF16)
                qcols = [q_ref[rows, (2 * h + c) * LANES:(2 * h + c + 1) * LANES] for c in range(2)]
                s_all = lax.dot_general(jnp.concatenate([k_lo, k_hi], axis=0), jnp.concatenate(qcols, axis=0),
                                        _NT, preferred_element_type=F32)
                for g in range(GROUP):
                    head = h * GROUP + g
                    half, c = g % 2, g // 2
                    s = s_all[half * N_KEYS:(half + 1) * N_KEYS, c * ATT_BLOCK:(c + 1) * ATT_BLOCK]
                    bias = bias_ref[head]
                    if pen_prev is not None or pen_next is not None:
                        pieces = [bias[0:ATT_BLOCK], bias[ATT_BLOCK:2 * ATT_BLOCK],
                                  bias[2 * ATT_BLOCK:3 * ATT_BLOCK], bias[3 * ATT_BLOCK:]]
                        if pen_prev is not None:
                            pieces[0] = pieces[0] + pen_prev
                        if pen_next is not None:
                            pieces[2] = pieces[2] + pen_next
                        bias = jnp.concatenate(pieces, axis=0)
                    s = s + bias
                    sink = sink_ref[head] * LOG2E
                    mx = jnp.maximum(jnp.max(s, axis=0, keepdims=True), sink)
                    p = jnp.exp2(s - mx)
                    denom = jnp.sum(p, axis=0, keepdims=True) + jnp.exp2(sink - mx)
                    probs.append(p.astype(BF16))
                    inv_denoms.append(1.0 / denom)
            o_t = lax.dot_general(vcol, jnp.concatenate(probs, axis=1), _TN, preferred_element_type=F32)
            for e in range(HEADS_PER_GROUP):
                h = m * HEADS_PER_GROUP + e
                for c in range(2):
                    pair = []
                    for half in range(2):
                        idx = e * GROUP + 2 * c + half
                        pair.append(o_t[e * HEAD_DIM:(e + 1) * HEAD_DIM, idx * ATT_BLOCK:(idx + 1) * ATT_BLOCK]
                                    * inv_denoms[idx])
                    out = jnp.concatenate(pair, axis=0).T
                    o_ref[rows, (2 * h + c) * LANES:(2 * h + c + 1) * LANES] = out.astype(BF16)


def _attn(p, p_meta, sink, batch, seq, qb):
    rows = batch * seq
    assert seq % qb == 0 and qb % ATT_BLOCK == 0
    nb = seq // qb
    per = qb // ATT_BLOCK
    last = rows // ATT_BLOCK - 1
    q_col, kv_col = PB_AQ, PB_KV

    return pl.pallas_call(
        functools.partial(_attn_kernel, seq=seq, qb=qb),
        grid=(batch, nb),
        in_specs=[
            pl.BlockSpec(memory_space=pltpu.SMEM),
            pl.BlockSpec((qb, ATT_WIDTH), lambda b, n: (b * nb + n, q_col)),
            pl.BlockSpec((qb, 2 * KV_WIDTH), lambda b, n: (b * nb + n, kv_col)),
            pl.BlockSpec((ATT_BLOCK, 2 * KV_WIDTH),
                         lambda b, n: (jnp.maximum((b * nb + n) * per - 1, 0), kv_col)),
            pl.BlockSpec((ATT_BLOCK, 2 * KV_WIDTH),
                         lambda b, n: (jnp.minimum((b * nb + n + 1) * per, last), kv_col)),
            pl.BlockSpec((N_META, 2 * KV_WIDTH), lambda b, n: (0, kv_col)),
        ],
        out_specs=pl.BlockSpec((qb, ATT_WIDTH), lambda b, n: (b * nb + n, 0)),
        out_shape=jax.ShapeDtypeStruct((rows, ATT_WIDTH), BF16),
        scratch_shapes=[pltpu.VMEM((N_Q_HEADS, N_KEYS, ATT_BLOCK), F32)],
        compiler_params=pltpu.CompilerParams(
            dimension_semantics=("arbitrary", "arbitrary"), vmem_limit_bytes=VMEM_LIMIT_BYTES),
        name="window_attn",
    )(sink, p, p, p, p, p_meta)


def _merge_kernel(of_ref, ob_ref, g_ref, att_ref, gt0_ref, gt1_ref, gt2_ref, gt3_ref, gt4_ref, x_ref,
                  wa_ref, wb_ref, wo_ref, hgain_ref, gpost_ref, gpre_ref, x1_ref, h2_ref, *, nsplit):
    sub = x_ref.shape[0] // nsplit
    half = P_GATE_START % IN_TILE
    for r in range(nsplit):
        rows = slice(r * sub, (r + 1) * sub)
        gate_a = jnp.concatenate([gt0_ref[rows, half:], gt1_ref[rows, :], gt2_ref[rows, :half]], axis=1)
        gate_b = jnp.concatenate([gt2_ref[rows, half:], gt3_ref[rows, :], gt4_ref[rows, :half]], axis=1)
        y = gate_b.astype(F32) * jnp.dot(att_ref[rows, :], wb_ref[...], preferred_element_type=F32)
        o = of_ref[rows, :] + ob_ref[rows, :]
        parts = []
        for h in range(HG_HEADS):
            oh = o[:, h * HG_DV:(h + 1) * HG_DV]
            parts.append(oh * _rms_scale(oh))
        a = jnp.concatenate(parts, axis=1) * hgain_ref[...] * g_ref[rows, :].astype(F32)
        y = y + gate_a.astype(F32) * jnp.dot(a.astype(BF16), wa_ref[...], preferred_element_type=F32)
        z = jnp.dot(y.astype(BF16), wo_ref[...], preferred_element_type=F32)
        x1 = x_ref[rows, :] + z * _rms_scale(z) * gpost_ref[...]
        x1_ref[rows, :] = x1
        h2_ref[rows, :] = (x1 * _rms_scale(x1) * gpre_ref[...]).astype(BF16)


def _merge(o_f, o_b, p, att, x, wa, wb, wo, hgain, gpost, gpre, tm):
    rows = x.shape[0]
    assert rows % tm == 0
    row = lambda c: (lambda i: (i, c))
    const = lambda i: (0, 0)
    single = pl.Buffered(1)
    nsplit = 2 if tm % 256 == 0 else 1
    return pl.pallas_call(
        functools.partial(_merge_kernel, nsplit=nsplit),
        grid=(rows // tm,),
        in_specs=[
            pl.BlockSpec((tm, HG_WIDTH), row(0)),
            pl.BlockSpec((tm, HG_WIDTH), row(0)),
            pl.BlockSpec((tm, HG_WIDTH), row(PB_G)),
            pl.BlockSpec((tm, ATT_WIDTH), row(0)),
            *[pl.BlockSpec((tm, IN_TILE), row(t)) for t in PB_GATE_TILES],
            pl.BlockSpec((tm, D_MODEL), row(0)),
            pl.BlockSpec((HG_WIDTH, D_MODEL), const, pipeline_mode=single),
            pl.BlockSpec((ATT_WIDTH, D_MODEL), const, pipeline_mode=single),
            pl.BlockSpec((D_MODEL, D_MODEL), const, pipeline_mode=single),
            pl.BlockSpec((1, HG_WIDTH), const),
            pl.BlockSpec((1, D_MODEL), const),
            pl.BlockSpec((1, D_MODEL), const),
        ],
        out_specs=[
            pl.BlockSpec((tm, D_MODEL), row(0)),
            pl.BlockSpec((tm, D_MODEL), row(0)),
        ],
        out_shape=[
            jax.ShapeDtypeStruct((rows, D_MODEL), F32),
            jax.ShapeDtypeStruct((rows, D_MODEL), BF16),
        ],
        compiler_params=pltpu.CompilerParams(
            dimension_semantics=("arbitrary",), vmem_limit_bytes=VMEM_LIMIT_BYTES),
        name="merge_out",
    )(o_f, o_b, p, att, *([p] * len(PB_GATE_TILES)), x, wa, wb, wo, hgain, gpost, gpre)


FF_UP_TILE = 1024
FF_DOWN_TILE = 512


def _ffn_up_kernel(h_ref, w1_ref, u_ref, *, nsplit):
    sub = h_ref.shape[0] // nsplit
    w1 = w1_ref[...].astype(BF16)
    for r in range(nsplit):
        rows = slice(r * sub, (r + 1) * sub)
        u = jnp.maximum(jnp.dot(h_ref[rows, :], w1, preferred_element_type=F32), 0.0)
        u_ref[rows, :] = (u * u).astype(BF16)


def _ffn_up(h2, w1, tm):
    rows = h2.shape[0]
    assert rows % tm == 0
    nsplit = 4 if tm % (4 * 128) == 0 else 1
    return pl.pallas_call(
        functools.partial(_ffn_up_kernel, nsplit=nsplit),
        grid=(rows // tm, D_FF // FF_UP_TILE),
        in_specs=[
            pl.BlockSpec((tm, D_MODEL), lambda i, j: (i, 0)),
            pl.BlockSpec((D_MODEL, FF_UP_TILE), lambda i, j: (0, j)),
        ],
        out_specs=pl.BlockSpec((tm, FF_UP_TILE), lambda i, j: (i, j)),
        out_shape=jax.ShapeDtypeStruct((rows, D_FF), BF16),
        compiler_params=pltpu.CompilerParams(
            dimension_semantics=("arbitrary", "arbitrary"), vmem_limit_bytes=VMEM_LIMIT_BYTES),
        name="ffn_up",
    )(h2, w1)


def _ffn_down_kernel(u_ref, x_ref, w2_ref, g_ref, y_ref):
    tile = _serpentine(pl.program_id(0), pl.program_id(1), pl.num_programs(1))
    cols = pl.ds(pl.multiple_of(tile * FF_DOWN_TILE, FF_DOWN_TILE), FF_DOWN_TILE)
    last = pl.program_id(1) == pl.num_programs(1) - 1

    @pl.when(jnp.logical_not(last))
    def _():
        y_ref[:, cols] = jnp.dot(u_ref[...], w2_ref[...], preferred_element_type=F32)

    @pl.when(last)
    def _():
        half = y_ref.shape[0] // 2
        for r in range(2):
            rows = slice(r * half, (r + 1) * half)
            y_ref[rows, cols] = jnp.dot(u_ref[rows, :], w2_ref[...], preferred_element_type=F32)
            ff = y_ref[rows, :]
            y_ref[rows, :] = x_ref[rows, :] + ff * _rms_scale(ff) * g_ref[...]


def _ffn_down(u, x1, w2, gpost, tm):
    rows = x1.shape[0]
    assert rows % tm == 0
    return pl.pallas_call(
        _ffn_down_kernel,
        grid=(rows // tm, D_MODEL // FF_DOWN_TILE),
        in_specs=[
            pl.BlockSpec((tm, D_FF), lambda i, j: (i, 0)),
            pl.BlockSpec((tm, D_MODEL), lambda i, j: (i, 0)),
            pl.BlockSpec((D_FF, FF_DOWN_TILE), lambda i, j: (0, _serpentine(i, j, D_MODEL // FF_DOWN_TILE))),
            pl.BlockSpec((1, D_MODEL), lambda i, j: (0, 0)),
        ],
        out_specs=pl.BlockSpec((tm, D_MODEL), lambda i, j: (i, 0)),
        out_shape=jax.ShapeDtypeStruct((rows, D_MODEL), F32),
        compiler_params=pltpu.CompilerParams(
            dimension_semantics=("arbitrary", "arbitrary"), vmem_limit_bytes=VMEM_LIMIT_BYTES),
        name="ffn_down",
    )(u, x1, w2, gpost)


def _tile(n, pref):
    return pref if n % pref == 0 else math.gcd(n, pref)


def _trunk(x, meta_proj, w, batch, seq):
    rows = batch * seq
    xf = x.reshape(rows, D_MODEL)
    p, lf = _in_proj(xf, w["g_pre_mix"], w["w_in"], w["lb_logits"], _tile(rows, 1024))
    p_m, lf_m = meta_proj
    o_f, o_b = _scan(p, lf, p_m, lf_m, batch, seq, _tile(seq, 512))
    att = _attn(p, p_m, w["attn_sink"], batch, seq, _tile(seq, 1024))
    x1, h2 = _merge(o_f, o_b, p, att, xf, w["w_proj_hg"], w["w_proj_att"], w["w_out"],
                    w["hg_out_gain"], w["g_post_mix"], w["g_pre_ff"], _tile(rows, 256))
    u = _ffn_up(h2, w["w_ff1"], _tile(rows, 2048))
    y = _ffn_down(u, x1, w["w_ff2"], w["g_post_ff"], _tile(rows, 512))
    return y.reshape(batch, seq, D_MODEL)


def _prepare(meta_tokens, w_in, w_proj_hg, w_proj_att, w_out, w_ff1, w_ff2, g_pre_mix, g_post_mix,
             g_pre_ff, g_post_ff, lb_logits, hg_out_gain, attn_sink):
    w = {
        "w_in": w_in[0].astype(F32),
        "w_proj_hg": w_proj_hg[0].astype(BF16),
        "w_proj_att": w_proj_att[0].astype(BF16),
        "w_out": w_out[0].astype(BF16),
        "w_ff1": w_ff1[0].astype(F32),
        "w_ff2": w_ff2[0].astype(BF16),
        "g_pre_mix": g_pre_mix[0].reshape(1, D_MODEL).astype(F32),
        "g_post_mix": g_post_mix[0].reshape(1, D_MODEL).astype(F32),
        "g_pre_ff": g_pre_ff[0].reshape(1, D_MODEL).astype(F32),
        "g_post_ff": g_post_ff[0].reshape(1, D_MODEL).astype(F32),
        "lb_logits": lb_logits.reshape(2, 2 * HG_KDIM).astype(F32),
        "hg_out_gain": hg_out_gain[0].reshape(1, HG_WIDTH).astype(F32),
        "attn_sink": attn_sink[0].astype(F32),
    }
    meta_proj = _in_proj(meta_tokens.astype(F32), w["g_pre_mix"], w["w_in"], w["lb_logits"], N_META)
    return w, meta_proj


def kernel(x_prompt, x_sample, meta_tokens, w_in, w_proj_hg, w_proj_att, w_out, w_ff1, w_ff2, g_pre_mix, g_post_mix, g_pre_ff, g_post_ff, lb_logits, hg_out_gain, attn_sink):
    w, meta_proj = _prepare(meta_tokens, w_in, w_proj_hg, w_proj_att, w_out, w_ff1, w_ff2, g_pre_mix,
                            g_post_mix, g_pre_ff, g_post_ff, lb_logits, hg_out_gain, attn_sink)
    y_prompt = _trunk(x_prompt, meta_proj, w, x_prompt.shape[0], x_prompt.shape[1])
    y_sample = _trunk(x_sample, meta_proj, w, x_sample.shape[0], x_sample.shape[1])
    return (y_prompt, y_sample)
```

```python
import functools
import math

import jax
import jax.numpy as jnp
from jax import lax
from jax.experimental import pallas as pl
from jax.experimental.pallas import tpu as pltpu

F32 = jnp.float32
BF16 = jnp.bfloat16

D_MODEL = 2048
N_META = 16
HG_HEADS = 8
HG_DK = 128
HG_DV = 128
HG_KDIM = HG_HEADS * HG_DK
HG_WIDTH = HG_HEADS * HG_DV
HG_CHUNK = 64
N_Q_HEADS = 16
N_KV_HEADS = 4
HEAD_DIM = 64
GROUP = N_Q_HEADS // N_KV_HEADS
ATT_WIDTH = N_Q_HEADS * HEAD_DIM
KV_WIDTH = N_KV_HEADS * HEAD_DIM
WINDOW = 128
ATT_BLOCK = 128
D_FF = 4 * D_MODEL
EPS = 1e-6
LOG2E = math.log2(math.e)
IN_COLS = 3 * HG_KDIM + 2 * HG_WIDTH + ATT_WIDTH + 2 * KV_WIDTH + 2 * D_MODEL

VMEM_LIMIT_BYTES = 56 * 1024 * 1024

IN_TILE = 1024
W_F_TILE = (HG_KDIM + HG_WIDTH) // IN_TILE
N_W_TILES = pl.cdiv(IN_COLS, IN_TILE)
N_P_TILES = N_W_TILES - 2
P_COLS = IN_COLS - 2 * HG_KDIM
LAST_TILE_COLS = P_COLS - (N_P_TILES - 1) * IN_TILE
LF_COLS = 2 * HG_KDIM
PB_Q, PB_I, PB_G, PB_AQ = 0, 1, 2, 3
PB_KV = 4 * IN_TILE // (2 * KV_WIDTH)
P_GATE_START = 4 * IN_TILE + 2 * KV_WIDTH
PB_GATE_TILES = tuple(range(4, N_P_TILES))


def _sigmoid(x):
    return 1.0 / (1.0 + jnp.exp(-x))


def _rms_scale(x):
    return lax.rsqrt(jnp.mean(x * x, axis=-1, keepdims=True) + EPS)


def _log_forget(f, lb_logits):
    e = jnp.exp(lb_logits - jnp.max(lb_logits, axis=0, keepdims=True))
    lb = e[0:1] / jnp.sum(e, axis=0, keepdims=True)
    return jnp.log(lb + (1.0 - lb) * _sigmoid(f))


def _row_blocks(tm):
    if tm % (8 * 128) != 0:
        return [slice(0, tm)]
    return [slice(r * tm // 8, (r + 1) * tm // 8) for r in range(8)]


def _serpentine(i, j, n):
    return jnp.where(i % 2 == 0, j, n - 1 - j)


def _in_proj_kernel(x_ref, g_ref, w_ref, lbl_ref, p_ref, lf_ref, h_ref):
    first_step = pl.program_id(1) == 0
    j = _serpentine(pl.program_id(0), pl.program_id(1), N_W_TILES)
    row_blocks = _row_blocks(h_ref.shape[0])

    @pl.when(first_step)
    def _():
        x = x_ref[...]
        h_ref[...] = (x * _rms_scale(x) * g_ref[...]).astype(BF16)

    def tile(cond, out_ref, epilogue, width):
        @pl.when(cond)
        def _():
            w = w_ref[:, :width].astype(BF16)
            for rows in row_blocks:
                acc = jnp.dot(h_ref[rows, :], w, preferred_element_type=F32)
                out_ref[rows, :width] = epilogue(acc, width).astype(out_ref.dtype)

    def kv_and_gates(acc, width):
        split = P_GATE_START % IN_TILE
        return jnp.concatenate([acc[:, :split], _sigmoid(acc[:, split:])], axis=1)

    def log_forget(acc, width):
        return _log_forget(acc, lbl_ref[...])

    pure_gates = PB_GATE_TILES[0] + 1
    is_silu = (j == PB_Q) | (j == PB_G)
    scale = jnp.where(j == PB_Q, HG_DK ** -0.5, jnp.where(j == PB_AQ, HEAD_DIM ** -0.5 * LOG2E, 1.0))
    c_silu = jnp.where(is_silu, scale, 0.0)
    c_linear = jnp.where(is_silu, 0.0, scale)
    tile(j < PB_GATE_TILES[0], p_ref, lambda a, w: a * (c_silu * _sigmoid(a) + c_linear), IN_TILE)
    tile(j == PB_GATE_TILES[0], p_ref, kv_and_gates, IN_TILE)
    tile((j >= pure_gates) & (j < N_P_TILES - 1), p_ref, lambda a, w: _sigmoid(a), IN_TILE)
    tile(j == N_P_TILES - 1, p_ref, lambda a, w: _sigmoid(a), LAST_TILE_COLS)
    tile(j >= N_P_TILES, lf_ref, log_forget, IN_TILE)


def _in_proj(x, gain, w_bf16, lb_logits2, tm):
    rows = x.shape[0]
    assert rows % tm == 0

    def w_tile(i, j):
        t = _serpentine(i, j, N_W_TILES)
        return jnp.where(t < W_F_TILE, t, jnp.where(t < N_P_TILES, t + 2, t - N_P_TILES + W_F_TILE))

    def p_tile(i, j):
        return jnp.minimum(_serpentine(i, j, N_W_TILES), N_P_TILES - 1)

    def f_tile(i, j):
        return jnp.maximum(_serpentine(i, j, N_W_TILES) - N_P_TILES, 0)

    return pl.pallas_call(
        _in_proj_kernel,
        grid=(rows // tm, N_W_TILES),
        in_specs=[
            pl.BlockSpec((tm, D_MODEL), lambda i, j: (i, 0)),
            pl.BlockSpec((1, D_MODEL), lambda i, j: (0, 0)),
            pl.BlockSpec((D_MODEL, IN_TILE), lambda i, j: (0, w_tile(i, j))),
            pl.BlockSpec((2, IN_TILE), lambda i, j: (0, f_tile(i, j))),
        ],
        out_specs=[
            pl.BlockSpec((tm, IN_TILE), lambda i, j: (i, p_tile(i, j))),
            pl.BlockSpec((tm, IN_TILE), lambda i, j: (i, f_tile(i, j))),
        ],
        out_shape=[
            jax.ShapeDtypeStruct((rows, P_COLS), BF16),
            jax.ShapeDtypeStruct((rows, LF_COLS), F32),
        ],
        scratch_shapes=[pltpu.VMEM((tm, D_MODEL), BF16)],
        compiler_params=pltpu.CompilerParams(
            dimension_semantics=("arbitrary", "arbitrary"), vmem_limit_bytes=VMEM_LIMIT_BYTES),
        name="in_proj",
    )(x, gain, w_bf16, lb_logits2)


W_I_TILE = HG_KDIM // IN_TILE
W_KV_TILE = (3 * HG_KDIM + 2 * HG_WIDTH + ATT_WIDTH) // IN_TILE
META_W_TILES = (W_I_TILE, W_KV_TILE, W_F_TILE)


def _meta_proj_kernel(x_ref, g_ref, w_ref, lbl_ref, v_ref, kv_ref, lf_ref, h_ref):
    j = pl.program_id(0)

    @pl.when(j == 0)
    def _():
        x = x_ref[...]
        h_ref[...] = (x * _rms_scale(x) * g_ref[...]).astype(BF16)

    acc = jnp.dot(h_ref[...], w_ref[...].astype(BF16), preferred_element_type=F32)

    @pl.when(j == 0)
    def _():
        v_ref[...] = acc.astype(BF16)

    @pl.when(j == 1)
    def _():
        kv_ref[...] = acc.astype(BF16)

    @pl.when(j == 2)
    def _():
        lf_ref[...] = _log_forget(acc, lbl_ref[...])


def _meta_proj(meta_tokens, gain, w_f32, lb_logits2):
    i_tile, kv_tile, f_tile = META_W_TILES
    const = lambda j: (0, 0)
    return pl.pallas_call(
        _meta_proj_kernel,
        grid=(len(META_W_TILES),),
        in_specs=[
            pl.BlockSpec((N_META, D_MODEL), const),
            pl.BlockSpec((1, D_MODEL), const),
            pl.BlockSpec((D_MODEL, IN_TILE),
                         lambda j: (0, jnp.where(j == 0, i_tile, jnp.where(j == 1, kv_tile, f_tile)))),
            pl.BlockSpec((2, IN_TILE), const),
        ],
        out_specs=[pl.BlockSpec((N_META, IN_TILE), const)] * 3,
        out_shape=[
            jax.ShapeDtypeStruct((N_META, IN_TILE), BF16),
            jax.ShapeDtypeStruct((N_META, IN_TILE), BF16),
            jax.ShapeDtypeStruct((N_META, IN_TILE), F32),
        ],
        scratch_shapes=[pltpu.VMEM((N_META, D_MODEL), BF16)],
        compiler_params=pltpu.CompilerParams(
            dimension_semantics=("arbitrary",), vmem_limit_bytes=VMEM_LIMIT_BYTES),
        name="meta_proj",
    )(meta_tokens, gain, w_f32, lb_logits2)


def _tri(n, lower):
    r = lax.broadcasted_iota(jnp.int32, (n, n), 0)
    c = lax.broadcasted_iota(jnp.int32, (n, n), 1)
    return (r >= c) if lower else (r <= c)


def _cumsum_rows(tri_bf16, lf):
    width = lf.shape[1]
    hi = lf.astype(BF16)
    lo = (lf - hi.astype(F32)).astype(BF16)
    r = jnp.dot(tri_bf16, jnp.concatenate([hi, lo], axis=1), preferred_element_type=F32)
    return r[:, :width] + r[:, width:]


_NT = (((1,), (1,)), ((), ()))
_TN = (((0,), (0,)), ((), ()))


def _scan_kernel(qf_ref, vf_ref, lff_ref, qb_ref, vb_ref, lfb_ref, vm_ref, lfm_ref,
                 of_ref, ob_ref, sf_ref, sb_ref, qd_ref, ki_ref, ke_ref, dec_ref, *, nchunks):
    n = pl.program_id(1)
    c_len = HG_CHUNK
    masks = (_tri(c_len, True), _tri(c_len, False))
    tris = tuple(jnp.where(m, 1.0, 0.0).astype(BF16) for m in masks)
    end_rows = (c_len - 1, 0)
    q_refs, v_refs, lf_refs = (qf_ref, qb_ref), (vf_ref, vb_ref), (lff_ref, lfb_ref)
    o_refs, s_refs = (of_ref, ob_ref), (sf_ref, sb_ref)

    @pl.when(n == 0)
    def _():
        sb_ref[...] = jnp.zeros_like(sb_ref)
        tri_m = jnp.where(_tri(N_META, True), 1.0, 0.0).astype(BF16)
        lf = lfm_ref[...]
        cum = _cumsum_rows(tri_m, lf)
        k_end = ((1.0 - jnp.exp(lf)) * jnp.exp(cum[N_META - 1:N_META, :] - cum)).astype(BF16)
        for h in range(HG_HEADS):
            sl = slice(h * HG_DK, (h + 1) * HG_DK)
            sf_ref[h] = lax.dot_general(vm_ref[:, sl], k_end[:, sl], _TN, preferred_element_type=F32)

    for c in range(nchunks):
        rows = slice(c * c_len, (c + 1) * c_len)
        for d in range(2):
            lf = lf_refs[d][rows, :]
            cum = _cumsum_rows(tris[d], lf)
            b_end = cum[end_rows[d]:end_rows[d] + 1, :]
            k_inv = (1.0 - jnp.exp(lf)) * jnp.exp(-cum)
            dec = jnp.exp(b_end)
            qd_ref[d, rows, :] = (q_refs[d][rows, :].astype(F32) * jnp.exp(cum)).astype(BF16)
            ki_ref[d, rows, :] = k_inv.astype(BF16)
            ke_ref[d, rows, :] = (k_inv * dec).astype(BF16)
            dec_ref[d, c] = dec

    def body(c, carry):
        chunk = (c, nchunks - 1 - c)
        work = []
        for d in range(2):
            rows = pl.ds(pl.multiple_of(chunk[d] * c_len, c_len), c_len)
            dec = dec_ref[d, chunk[d]]
            for h in range(HG_HEADS):
                sl = slice(h * HG_DK, (h + 1) * HG_DK)
                qd = qd_ref[d, rows, sl]
                v = v_refs[d][rows, sl]
                st = s_refs[d][h]
                scores = lax.dot_general(qd, ki_ref[d, rows, sl], _NT, preferred_element_type=F32)
                inter = lax.dot_general(qd, st.astype(BF16), _NT, preferred_element_type=F32)
                inc = lax.dot_general(v, ke_ref[d, rows, sl], _TN, preferred_element_type=F32)
                s_refs[d][h] = st * dec[:, sl] + inc
                work.append((d, rows, sl, scores, inter, v))
        for d, rows, sl, scores, inter, v in work:
            intra = jnp.dot(jnp.where(masks[d], scores, 0.0).astype(BF16), v, preferred_element_type=F32)
            o_refs[d][rows, sl] = intra + inter
        return carry

    lax.fori_loop(0, nchunks, body, 0, unroll=2 if nchunks % 2 == 0 else 1)


def _scan(p, lf, v_meta, lf_meta, batch, seq, rb):
    rows = batch * seq
    assert seq % rb == 0 and rb % HG_CHUNK == 0
    nb = seq // rb

    def fwd(col):
        return lambda b, n: (b * nb + n, col)

    def bwd(col):
        return lambda b, n: (b * nb + nb - 1 - n, col)

    return pl.pallas_call(
        functools.partial(_scan_kernel, nchunks=rb // HG_CHUNK),
        grid=(batch, nb),
        in_specs=[
            pl.BlockSpec((rb, HG_KDIM), fwd(PB_Q)),
            pl.BlockSpec((rb, HG_WIDTH), fwd(PB_I)),
            pl.BlockSpec((rb, HG_KDIM), fwd(0)),
            pl.BlockSpec((rb, HG_KDIM), bwd(PB_Q)),
            pl.BlockSpec((rb, HG_WIDTH), bwd(PB_I)),
            pl.BlockSpec((rb, HG_KDIM), bwd(1)),
            pl.BlockSpec((N_META, HG_WIDTH), lambda b, n: (0, 0)),
            pl.BlockSpec((N_META, HG_KDIM), lambda b, n: (0, 0)),
        ],
        out_specs=[
            pl.BlockSpec((rb, HG_WIDTH), fwd(0)),
            pl.BlockSpec((rb, HG_WIDTH), bwd(0)),
        ],
        out_shape=[
            jax.ShapeDtypeStruct((rows, HG_WIDTH), F32),
            jax.ShapeDtypeStruct((rows, HG_WIDTH), F32),
        ],
        scratch_shapes=[
            pltpu.VMEM((HG_HEADS, HG_DV, HG_DK), F32),
            pltpu.VMEM((HG_HEADS, HG_DV, HG_DK), F32),
            pltpu.VMEM((2, rb, HG_KDIM), BF16),
            pltpu.VMEM((2, rb, HG_KDIM), BF16),
            pltpu.VMEM((2, rb, HG_KDIM), BF16),
            pltpu.VMEM((2, rb // HG_CHUNK, 1, HG_KDIM), F32),
        ],
        compiler_params=pltpu.CompilerParams(
            dimension_semantics=("arbitrary", "arbitrary"), vmem_limit_bytes=VMEM_LIMIT_BYTES),
        name="hgrn2_scan",
    )(p, p, lf, p, p, lf, v_meta, lf_meta)


N_KEYS = 3 * ATT_BLOCK + N_META


def _alibi_slope(head):
    return 2.0 ** (-8.0 * (head + 1) / N_Q_HEADS)


LANES = 128
HEADS_PER_GROUP = LANES // HEAD_DIM


def _attn_kernel(sink_ref, q_ref, kvc_ref, kvp_ref, kvn_ref, kvm_ref, o_ref, bias_ref, *, seq, qb):
    n = pl.program_id(1)
    nsub = qb // ATT_BLOCK
    neg_inf = -jnp.inf

    @pl.when((pl.program_id(0) == 0) & (n == 0))
    def _():
        u = lax.broadcasted_iota(jnp.int32, (N_KEYS, ATT_BLOCK), 0)
        r = lax.broadcasted_iota(jnp.int32, (N_KEYS, ATT_BLOCK), 1)
        dist = jnp.abs(u - ATT_BLOCK - r)
        is_meta = u >= 3 * ATT_BLOCK
        dist_f = dist.astype(F32)
        for head in range(N_Q_HEADS):
            inside = jnp.where(dist <= WINDOW, (-_alibi_slope(head) * LOG2E) * dist_f, neg_inf)
            bias_ref[head] = jnp.where(is_meta, 0.0, inside)

    lane = lax.broadcasted_iota(jnp.int32, (N_KEYS, LANES), 1)
    low_half = lane < HEAD_DIM
    for i in range(nsub):
        rows = slice(i * ATT_BLOCK, (i + 1) * ATT_BLOCK)
        kv_prev = kvp_ref[...] if i == 0 else kvc_ref[(i - 1) * ATT_BLOCK:i * ATT_BLOCK, :]
        kv_next = kvn_ref[...] if i == nsub - 1 else kvc_ref[(i + 1) * ATT_BLOCK:(i + 2) * ATT_BLOCK, :]
        kv_all = jnp.concatenate([kv_prev, kvc_ref[rows, :], kv_next, kvm_ref[...]], axis=0)
        pen_prev = jnp.where(n * qb + (i - 1) * ATT_BLOCK < 0, neg_inf, 0.0) if i == 0 else None
        pen_next = jnp.where(n * qb + (i + 1) * ATT_BLOCK >= seq, neg_inf, 0.0) if i == nsub - 1 else None
        for m in range(N_KV_HEADS // HEADS_PER_GROUP):
            kcol = kv_all[:, m * LANES:(m + 1) * LANES]
            vcol = kv_all[:, KV_WIDTH + m * LANES:KV_WIDTH + (m + 1) * LANES]
            kswap = jnp.concatenate([kcol[:, HEAD_DIM:], kcol[:, :HEAD_DIM]], axis=1)
            probs, inv_denoms = [], []
            for e in range(HEADS_PER_GROUP):
                h = m * HEADS_PER_GROUP + e
                k_lo = jnp.where(low_half, kcol if e == 0 else kswap, 0).astype(BF16)
                k_hi = jnp.where(low_half, 0, kswap if e == 0 else kcol).astype(BF16)
                qcols = [q_ref[rows, (2 * h + c) * LANES:(2 * h + c + 1) * LANES] for c in range(2)]
                s_all = lax.dot_general(jnp.concatenate([k_lo, k_hi], axis=0), jnp.concatenate(qcols, axis=0),
                                        _NT, preferred_element_type=F32)
                for g in range(GROUP):
                    head = h * GROUP + g
                    half, c = g % 2, g // 2
                    s = s_all[half * N_KEYS:(half + 1) * N_KEYS, c * ATT_BLOCK:(c + 1) * ATT_BLOCK]
                    bias = bias_ref[head]
                    if pen_prev is not None or pen_next is not None:
                        pieces = [bias[0:ATT_BLOCK], bias[ATT_BLOCK:2 * ATT_BLOCK],
                                  bias[2 * ATT_BLOCK:3 * ATT_BLOCK], bias[3 * ATT_BLOCK:]]
                        if pen_prev is not None:
                            pieces[0] = pieces[0] + pen_prev
                        if pen_next is not None:
                            pieces[2] = pieces[2] + pen_next
                        bias = jnp.concatenate(pieces, axis=0)
                    s = s + bias
                    sink = sink_ref[head] * LOG2E
                    mx = jnp.maximum(jnp.max(s, axis=0, keepdims=True), sink)
                    p = jnp.exp2(s - mx)
                    denom = jnp.sum(p, axis=0, keepdims=True) + jnp.exp2(sink - mx)
                    probs.append(p.astype(BF16))
                    inv_denoms.append(1.0 / denom)
            o_t = lax.dot_general(vcol, jnp.concatenate(probs, axis=1), _TN, preferred_element_type=F32)
            for e in range(HEADS_PER_GROUP):
                h = m * HEADS_PER_GROUP + e
                for c in range(2):
                    pair = []
                    for half in range(2):
                        idx = e * GROUP + 2 * c + half
                        pair.append(o_t[e * HEAD_DIM:(e + 1) * HEAD_DIM, idx * ATT_BLOCK:(idx + 1) * ATT_BLOCK]
                                    * inv_denoms[idx])
                    out = jnp.concatenate(pair, axis=0).T
                    o_ref[rows, (2 * h + c) * LANES:(2 * h + c + 1) * LANES] = out.astype(BF16)


def _attn(p, kv_meta, sink, batch, seq, qb):
    rows = batch * seq
    assert seq % qb == 0 and qb % ATT_BLOCK == 0
    nb = seq // qb
    per = qb // ATT_BLOCK
    last = rows // ATT_BLOCK - 1
    q_col, kv_col = PB_AQ, PB_KV

    return pl.pallas_call(
        functools.partial(_attn_kernel, seq=seq, qb=qb),
        grid=(batch, nb),
        in_specs=[
            pl.BlockSpec(memory_space=pltpu.SMEM),
            pl.BlockSpec((qb, ATT_WIDTH), lambda b, n: (b * nb + n, q_col)),
            pl.BlockSpec((qb, 2 * KV_WIDTH), lambda b, n: (b * nb + n, kv_col)),
            pl.BlockSpec((ATT_BLOCK, 2 * KV_WIDTH),
                         lambda b, n: (jnp.maximum((b * nb + n) * per - 1, 0), kv_col)),
            pl.BlockSpec((ATT_BLOCK, 2 * KV_WIDTH),
                         lambda b, n: (jnp.minimum((b * nb + n + 1) * per, last), kv_col)),
            pl.BlockSpec((N_META, 2 * KV_WIDTH), lambda b, n: (0, 0)),
        ],
        out_specs=pl.BlockSpec((qb, ATT_WIDTH), lambda b, n: (b * nb + n, 0)),
        out_shape=jax.ShapeDtypeStruct((rows, ATT_WIDTH), BF16),
        scratch_shapes=[pltpu.VMEM((N_Q_HEADS, N_KEYS, ATT_BLOCK), F32)],
        compiler_params=pltpu.CompilerParams(
            dimension_semantics=("arbitrary", "arbitrary"), vmem_limit_bytes=VMEM_LIMIT_BYTES),
        name="window_attn",
    )(sink, p, p, p, p, kv_meta)


def _merge_kernel(of_ref, ob_ref, g_ref, att_ref, gt0_ref, gt1_ref, gt2_ref, gt3_ref, gt4_ref, x_ref,
                  wa_ref, wb_ref, wo_ref, hgain_ref, gpost_ref, gpre_ref, x1_ref, h2_ref, *, nsplit):
    sub = x_ref.shape[0] // nsplit
    half = P_GATE_START % IN_TILE
    for r in range(nsplit):
        rows = slice(r * sub, (r + 1) * sub)
        gate_a = jnp.concatenate([gt0_ref[rows, half:], gt1_ref[rows, :], gt2_ref[rows, :half]], axis=1)
        gate_b = jnp.concatenate([gt2_ref[rows, half:], gt3_ref[rows, :], gt4_ref[rows, :half]], axis=1)
        y = gate_b.astype(F32) * jnp.dot(att_ref[rows, :], wb_ref[...], preferred_element_type=F32)
        o = of_ref[rows, :] + ob_ref[rows, :]
        parts = []
        for h in range(HG_HEADS):
            oh = o[:, h * HG_DV:(h + 1) * HG_DV]
            parts.append(oh * _rms_scale(oh))
        a = jnp.concatenate(parts, axis=1) * hgain_ref[...] * g_ref[rows, :].astype(F32)
        y = y + gate_a.astype(F32) * jnp.dot(a.astype(BF16), wa_ref[...], preferred_element_type=F32)
        z = jnp.dot(y.astype(BF16), wo_ref[...], preferred_element_type=F32)
        x1 = x_ref[rows, :] + z * _rms_scale(z) * gpost_ref[...]
        x1_ref[rows, :] = x1
        h2_ref[rows, :] = (x1 * _rms_scale(x1) * gpre_ref[...]).astype(BF16)


def _merge(o_f, o_b, p, att, x, wa, wb, wo, hgain, gpost, gpre, tm):
    rows = x.shape[0]
    assert rows % tm == 0
    row = lambda c: (lambda i: (i, c))
    const = lambda i: (0, 0)
    single = pl.Buffered(1)
    nsplit = 2 if tm % 256 == 0 else 1
    return pl.pallas_call(
        functools.partial(_merge_kernel, nsplit=nsplit),
        grid=(rows // tm,),
        in_specs=[
            pl.BlockSpec((tm, HG_WIDTH), row(0)),
            pl.BlockSpec((tm, HG_WIDTH), row(0)),
            pl.BlockSpec((tm, HG_WIDTH), row(PB_G)),
            pl.BlockSpec((tm, ATT_WIDTH), row(0)),
            *[pl.BlockSpec((tm, IN_TILE), row(t)) for t in PB_GATE_TILES],
            pl.BlockSpec((tm, D_MODEL), row(0)),
            pl.BlockSpec((HG_WIDTH, D_MODEL), const, pipeline_mode=single),
            pl.BlockSpec((ATT_WIDTH, D_MODEL), const, pipeline_mode=single),
            pl.BlockSpec((D_MODEL, D_MODEL), const, pipeline_mode=single),
            pl.BlockSpec((1, HG_WIDTH), const),
            pl.BlockSpec((1, D_MODEL), const),
            pl.BlockSpec((1, D_MODEL), const),
        ],
        out_specs=[
            pl.BlockSpec((tm, D_MODEL), row(0)),
            pl.BlockSpec((tm, D_MODEL), row(0)),
        ],
        out_shape=[
            jax.ShapeDtypeStruct((rows, D_MODEL), F32),
            jax.ShapeDtypeStruct((rows, D_MODEL), BF16),
        ],
        compiler_params=pltpu.CompilerParams(
            dimension_semantics=("arbitrary",), vmem_limit_bytes=VMEM_LIMIT_BYTES),
        name="merge_out",
    )(o_f, o_b, p, att, *([p] * len(PB_GATE_TILES)), x, wa, wb, wo, hgain, gpost, gpre)


FF_UP_TILE = 1024
FF_DOWN_TILE = 512


def _ffn_up_kernel(h_ref, w1_ref, u_ref, *, nsplit):
    sub = h_ref.shape[0] // nsplit
    w1 = w1_ref[...].astype(BF16)
    for r in range(nsplit):
        rows = slice(r * sub, (r + 1) * sub)
        u = jnp.maximum(jnp.dot(h_ref[rows, :], w1, preferred_element_type=F32), 0.0)
        u_ref[rows, :] = (u * u).astype(BF16)


def _ffn_up(h2, w1, tm):
    rows = h2.shape[0]
    assert rows % tm == 0
    nsplit = 4 if tm % (4 * 128) == 0 else 1
    n_tiles = D_FF // FF_UP_TILE
    return pl.pallas_call(
        functools.partial(_ffn_up_kernel, nsplit=nsplit),
        grid=(rows // tm, n_tiles),
        in_specs=[
            pl.BlockSpec((tm, D_MODEL), lambda i, j: (i, 0)),
            pl.BlockSpec((D_MODEL, FF_UP_TILE), lambda i, j: (0, _serpentine(i, j, n_tiles))),
        ],
        out_specs=pl.BlockSpec((tm, FF_UP_TILE), lambda i, j: (i, _serpentine(i, j, n_tiles))),
        out_shape=jax.ShapeDtypeStruct((rows, D_FF), BF16),
        compiler_params=pltpu.CompilerParams(
            dimension_semantics=("arbitrary", "arbitrary"), vmem_limit_bytes=VMEM_LIMIT_BYTES),
        name="ffn_up",
    )(h2, w1)


def _ffn_down_kernel(u_ref, x_ref, w2_ref, g_ref, y_ref):
    tile = _serpentine(pl.program_id(0), pl.program_id(1), pl.num_programs(1))
    cols = pl.ds(pl.multiple_of(tile * FF_DOWN_TILE, FF_DOWN_TILE), FF_DOWN_TILE)
    last = pl.program_id(1) == pl.num_programs(1) - 1

    @pl.when(jnp.logical_not(last))
    def _():
        y_ref[:, cols] = jnp.dot(u_ref[...], w2_ref[...], preferred_element_type=F32)

    @pl.when(last)
    def _():
        half = y_ref.shape[0] // 2
        for r in range(2):
            rows = slice(r * half, (r + 1) * half)
            y_ref[rows, cols] = jnp.dot(u_ref[rows, :], w2_ref[...], preferred_element_type=F32)
            ff = y_ref[rows, :]
            y_ref[rows, :] = x_ref[rows, :] + ff * _rms_scale(ff) * g_ref[...]


def _ffn_down(u, x1, w2, gpost, tm):
    rows = x1.shape[0]
    assert rows % tm == 0
    return pl.pallas_call(
        _ffn_down_kernel,
        grid=(rows // tm, D_MODEL // FF_DOWN_TILE),
        in_specs=[
            pl.BlockSpec((tm, D_FF), lambda i, j: (i, 0)),
            pl.BlockSpec((tm, D_MODEL), lambda i, j: (i, 0)),
            pl.BlockSpec((D_FF, FF_DOWN_TILE), lambda i, j: (0, _serpentine(i, j, D_MODEL // FF_DOWN_TILE))),
            pl.BlockSpec((1, D_MODEL), lambda i, j: (0, 0)),
        ],
        out_specs=pl.BlockSpec((tm, D_MODEL), lambda i, j: (i, 0)),
        out_shape=jax.ShapeDtypeStruct((rows, D_MODEL), F32),
        compiler_params=pltpu.CompilerParams(
            dimension_semantics=("arbitrary", "arbitrary"), vmem_limit_bytes=VMEM_LIMIT_BYTES),
        name="ffn_down",
    )(u, x1, w2, gpost)


def _tile(n, pref):
    return pref if n % pref == 0 else math.gcd(n, pref)


def _trunk(x, meta_proj, w, batch, seq):
    rows = batch * seq
    xf = x.reshape(rows, D_MODEL)
    p, lf = _in_proj(xf, w["g_pre_mix"], w["w_in"], w["lb_logits"], _tile(rows, 1024))
    v_m, kv_m, lf_m = meta_proj
    o_f, o_b = _scan(p, lf, v_m, lf_m, batch, seq, _tile(seq, 512))
    att = _attn(p, kv_m, w["attn_sink"], batch, seq, _tile(seq, 1024))
    x1, h2 = _merge(o_f, o_b, p, att, xf, w["w_proj_hg"], w["w_proj_att"], w["w_out"],
                    w["hg_out_gain"], w["g_post_mix"], w["g_pre_ff"], _tile(rows, 256))
    u = _ffn_up(h2, w["w_ff1"], _tile(rows, 2048))
    y = _ffn_down(u, x1, w["w_ff2"], w["g_post_ff"], _tile(rows, 512))
    return y.reshape(batch, seq, D_MODEL)


def _prepare(meta_tokens, w_in, w_proj_hg, w_proj_att, w_out, w_ff1, w_ff2, g_pre_mix, g_post_mix,
             g_pre_ff, g_post_ff, lb_logits, hg_out_gain, attn_sink):
    w = {
        "w_in": w_in[0].astype(F32),
        "w_proj_hg": w_proj_hg[0].astype(BF16),
        "w_proj_att": w_proj_att[0].astype(BF16),
        "w_out": w_out[0].astype(BF16),
        "w_ff1": w_ff1[0].astype(F32),
        "w_ff2": w_ff2[0].astype(BF16),
        "g_pre_mix": g_pre_mix[0].reshape(1, D_MODEL).astype(F32),
        "g_post_mix": g_post_mix[0].reshape(1, D_MODEL).astype(F32),
        "g_pre_ff": g_pre_ff[0].reshape(1, D_MODEL).astype(F32),
        "g_post_ff": g_post_ff[0].reshape(1, D_MODEL).astype(F32),
        "lb_logits": lb_logits.reshape(2, 2 * HG_KDIM).astype(F32),
        "hg_out_gain": hg_out_gain[0].reshape(1, HG_WIDTH).astype(F32),
        "attn_sink": attn_sink[0].astype(F32),
    }
    meta_proj = _meta_proj(meta_tokens.astype(F32), w["g_pre_mix"], w["w_in"], w["lb_logits"])
    return w, meta_proj


def kernel(x_prompt, x_sample, meta_tokens, w_in, w_proj_hg, w_proj_att, w_out, w_ff1, w_ff2, g_pre_mix, g_post_mix, g_pre_ff, g_post_ff, lb_logits, hg_out_gain, attn_sink):
    w, meta_proj = _prepare(meta_tokens, w_in, w_proj_hg, w_proj_att, w_out, w_ff1, w_ff2, g_pre_mix,
                            g_post_mix, g_pre_ff, g_post_ff, lb_logits, hg_out_gain, attn_sink)
    y_prompt = _trunk(x_prompt, meta_proj, w, x_prompt.shape[0], x_prompt.shape[1])
    y_sample = _trunk(x_sample, meta_proj, w, x_sample.shape[0], x_sample.shape[1])
    return (y_prompt, y_sample)
```

```python
import functools
import math

import jax
import jax.numpy as jnp
from jax import lax
from jax.experimental import pallas as pl
from jax.experimental.pallas import tpu as pltpu

F32 = jnp.float32
BF16 = jnp.bfloat16

D_MODEL = 2048
N_META = 16
HG_HEADS = 8
HG_DK = 128
HG_DV = 128
HG_KDIM = HG_HEADS * HG_DK
HG_WIDTH = HG_HEADS * HG_DV
HG_CHUNK = 64
N_Q_HEADS = 16
N_KV_HEADS = 4
HEAD_DIM = 64
GROUP = N_Q_HEADS // N_KV_HEADS
ATT_WIDTH = N_Q_HEADS * HEAD_DIM
KV_WIDTH = N_KV_HEADS * HEAD_DIM
WINDOW = 128
ATT_BLOCK = 128
D_FF = 4 * D_MODEL
EPS = 1e-6
LOG2E = math.log2(math.e)
IN_COLS = 3 * HG_KDIM + 2 * HG_WIDTH + ATT_WIDTH + 2 * KV_WIDTH + 2 * D_MODEL

VMEM_LIMIT_BYTES = 56 * 1024 * 1024

IN_TILE = 1024
W_F_TILE = (HG_KDIM + HG_WIDTH) // IN_TILE
N_W_TILES = pl.cdiv(IN_COLS, IN_TILE)
N_P_TILES = N_W_TILES - 2
P_COLS = IN_COLS - 2 * HG_KDIM
LAST_TILE_COLS = P_COLS - (N_P_TILES - 1) * IN_TILE
LF_COLS = 2 * HG_KDIM
PB_Q, PB_I, PB_G, PB_AQ = 0, 1, 2, 3
PB_KV = 4 * IN_TILE // (2 * KV_WIDTH)
P_GATE_START = 4 * IN_TILE + 2 * KV_WIDTH
PB_GATE_TILES = tuple(range(4, N_P_TILES))


def _sigmoid(x):
    return 1.0 / (1.0 + jnp.exp(-x))


def _rms_scale(x):
    return lax.rsqrt(jnp.mean(x * x, axis=-1, keepdims=True) + EPS)


def _log_forget(f, lb_logits):
    e = jnp.exp(lb_logits - jnp.max(lb_logits, axis=0, keepdims=True))
    lb = e[0:1] / jnp.sum(e, axis=0, keepdims=True)
    return jnp.log(lb + (1.0 - lb) * _sigmoid(f))


def _row_blocks(tm):
    if tm % (8 * 128) != 0:
        return [slice(0, tm)]
    return [slice(r * tm // 8, (r + 1) * tm // 8) for r in range(8)]


def _serpentine(i, j, n):
    return jnp.where(i % 2 == 0, j, n - 1 - j)


def _in_proj_kernel(x_ref, g_ref, w_ref, lbl_ref, p_ref, lf_ref, h_ref):
    first_step = pl.program_id(1) == 0
    j = _serpentine(pl.program_id(0), pl.program_id(1), N_W_TILES)
    row_blocks = _row_blocks(h_ref.shape[0])

    @pl.when(first_step)
    def _():
        x = x_ref[...]
        h_ref[...] = (x * _rms_scale(x) * g_ref[...]).astype(BF16)

    def tile(cond, out_ref, epilogue, width):
        @pl.when(cond)
        def _():
            w = w_ref[:, :width].astype(BF16)
            for rows in row_blocks:
                acc = jnp.dot(h_ref[rows, :], w, preferred_element_type=F32)
                out_ref[rows, :width] = epilogue(acc, width).astype(out_ref.dtype)

    def kv_and_gates(acc, width):
        split = P_GATE_START % IN_TILE
        return jnp.concatenate([acc[:, :split], _sigmoid(acc[:, split:])], axis=1)

    def log_forget(acc, width):
        return _log_forget(acc, lbl_ref[...])

    pure_gates = PB_GATE_TILES[0] + 1
    is_silu = (j == PB_Q) | (j == PB_G)
    scale = jnp.where(j == PB_Q, HG_DK ** -0.5, jnp.where(j == PB_AQ, HEAD_DIM ** -0.5 * LOG2E, 1.0))
    c_silu = jnp.where(is_silu, scale, 0.0)
    c_linear = jnp.where(is_silu, 0.0, scale)
    tile(j < PB_GATE_TILES[0], p_ref, lambda a, w: a * (c_silu * _sigmoid(a) + c_linear), IN_TILE)
    tile(j == PB_GATE_TILES[0], p_ref, kv_and_gates, IN_TILE)
    tile((j >= pure_gates) & (j < N_P_TILES - 1), p_ref, lambda a, w: _sigmoid(a), IN_TILE)
    tile(j == N_P_TILES - 1, p_ref, lambda a, w: _sigmoid(a), LAST_TILE_COLS)
    tile(j >= N_P_TILES, lf_ref, log_forget, IN_TILE)


def _in_proj(x, gain, w_bf16, lb_logits2, tm):
    rows = x.shape[0]
    assert rows % tm == 0

    def w_tile(i, j):
        t = _serpentine(i, j, N_W_TILES)
        return jnp.where(t < W_F_TILE, t, jnp.where(t < N_P_TILES, t + 2, t - N_P_TILES + W_F_TILE))

    def p_tile(i, j):
        return jnp.minimum(_serpentine(i, j, N_W_TILES), N_P_TILES - 1)

    def f_tile(i, j):
        return jnp.maximum(_serpentine(i, j, N_W_TILES) - N_P_TILES, 0)

    return pl.pallas_call(
        _in_proj_kernel,
        grid=(rows // tm, N_W_TILES),
        in_specs=[
            pl.BlockSpec((tm, D_MODEL), lambda i, j: (i, 0)),
            pl.BlockSpec((1, D_MODEL), lambda i, j: (0, 0)),
            pl.BlockSpec((D_MODEL, IN_TILE), lambda i, j: (0, w_tile(i, j))),
            pl.BlockSpec((2, IN_TILE), lambda i, j: (0, f_tile(i, j))),
        ],
        out_specs=[
            pl.BlockSpec((tm, IN_TILE), lambda i, j: (i, p_tile(i, j))),
            pl.BlockSpec((tm, IN_TILE), lambda i, j: (i, f_tile(i, j))),
        ],
        out_shape=[
            jax.ShapeDtypeStruct((rows, P_COLS), BF16),
            jax.ShapeDtypeStruct((rows, LF_COLS), F32),
        ],
        scratch_shapes=[pltpu.VMEM((tm, D_MODEL), BF16)],
        compiler_params=pltpu.CompilerParams(
            dimension_semantics=("arbitrary", "arbitrary"), vmem_limit_bytes=VMEM_LIMIT_BYTES),
        name="in_proj",
    )(x, gain, w_bf16, lb_logits2)


W_I_TILE = HG_KDIM // IN_TILE
W_KV_TILE = (3 * HG_KDIM + 2 * HG_WIDTH + ATT_WIDTH) // IN_TILE
META_W_TILES = (W_I_TILE, W_KV_TILE, W_F_TILE)


def _meta_proj_kernel(x_ref, g_ref, w_ref, lbl_ref, v_ref, kv_ref, lf_ref, h_ref):
    j = pl.program_id(0)

    @pl.when(j == 0)
    def _():
        x = x_ref[...]
        h_ref[...] = (x * _rms_scale(x) * g_ref[...]).astype(BF16)

    acc = jnp.dot(h_ref[...], w_ref[...].astype(BF16), preferred_element_type=F32)

    @pl.when(j == 0)
    def _():
        v_ref[...] = acc.astype(BF16)

    @pl.when(j == 1)
    def _():
        kv_ref[...] = acc.astype(BF16)

    @pl.when(j == 2)
    def _():
        lf_ref[...] = _log_forget(acc, lbl_ref[...])


def _meta_proj(meta_tokens, gain, w_f32, lb_logits2):
    i_tile, kv_tile, f_tile = META_W_TILES
    const = lambda j: (0, 0)
    return pl.pallas_call(
        _meta_proj_kernel,
        grid=(len(META_W_TILES),),
        in_specs=[
            pl.BlockSpec((N_META, D_MODEL), const),
            pl.BlockSpec((1, D_MODEL), const),
            pl.BlockSpec((D_MODEL, IN_TILE),
                         lambda j: (0, jnp.where(j == 0, i_tile, jnp.where(j == 1, kv_tile, f_tile)))),
            pl.BlockSpec((2, IN_TILE), const),
        ],
        out_specs=[pl.BlockSpec((N_META, IN_TILE), const)] * 3,
        out_shape=[
            jax.ShapeDtypeStruct((N_META, IN_TILE), BF16),
            jax.ShapeDtypeStruct((N_META, IN_TILE), BF16),
            jax.ShapeDtypeStruct((N_META, IN_TILE), F32),
        ],
        scratch_shapes=[pltpu.VMEM((N_META, D_MODEL), BF16)],
        compiler_params=pltpu.CompilerParams(
            dimension_semantics=("arbitrary",), vmem_limit_bytes=VMEM_LIMIT_BYTES),
        name="meta_proj",
    )(meta_tokens, gain, w_f32, lb_logits2)


def _tri(n, lower):
    r = lax.broadcasted_iota(jnp.int32, (n, n), 0)
    c = lax.broadcasted_iota(jnp.int32, (n, n), 1)
    return (r >= c) if lower else (r <= c)


def _cumsum_rows(tri_bf16, lf):
    width = lf.shape[1]
    hi = lf.astype(BF16)
    lo = (lf - hi.astype(F32)).astype(BF16)
    r = jnp.dot(tri_bf16, jnp.concatenate([hi, lo], axis=1), preferred_element_type=F32)
    return r[:, :width] + r[:, width:]


_NT = (((1,), (1,)), ((), ()))
_TN = (((0,), (0,)), ((), ()))


def _scan_kernel(qf_ref, vf_ref, lff_ref, qb_ref, vb_ref, lfb_ref, vm_ref, lfm_ref,
                 of_ref, ob_ref, sf_ref, sb_ref, qd_ref, ki_ref, ke_ref, dec_ref, *, nchunks):
    n = pl.program_id(1)
    c_len = HG_CHUNK
    masks = (_tri(c_len, True), _tri(c_len, False))
    tris = tuple(jnp.where(m, 1.0, 0.0).astype(BF16) for m in masks)
    end_rows = (c_len - 1, 0)
    q_refs, v_refs, lf_refs = (qf_ref, qb_ref), (vf_ref, vb_ref), (lff_ref, lfb_ref)
    o_refs, s_refs = (of_ref, ob_ref), (sf_ref, sb_ref)

    @pl.when(n == 0)
    def _():
        sb_ref[...] = jnp.zeros_like(sb_ref)
        tri_m = jnp.where(_tri(N_META, True), 1.0, 0.0).astype(BF16)
        lf = lfm_ref[...]
        cum = _cumsum_rows(tri_m, lf)
        k_end = ((1.0 - jnp.exp(lf)) * jnp.exp(cum[N_META - 1:N_META, :] - cum)).astype(BF16)
        for h in range(HG_HEADS):
            sl = slice(h * HG_DK, (h + 1) * HG_DK)
            sf_ref[h] = lax.dot_general(vm_ref[:, sl], k_end[:, sl], _TN, preferred_element_type=F32)

    for c in range(nchunks):
        rows = slice(c * c_len, (c + 1) * c_len)
        for d in range(2):
            lf = lf_refs[d][rows, :]
            cum = _cumsum_rows(tris[d], lf)
            b_end = cum[end_rows[d]:end_rows[d] + 1, :]
            k_inv = (1.0 - jnp.exp(lf)) * jnp.exp(-cum)
            dec = jnp.exp(b_end)
            qd_ref[d, rows, :] = (q_refs[d][rows, :].astype(F32) * jnp.exp(cum)).astype(BF16)
            ki_ref[d, rows, :] = k_inv.astype(BF16)
            ke_ref[d, rows, :] = (k_inv * dec).astype(BF16)
            dec_ref[d, c] = dec

    def body(c, carry):
        chunk = (c, nchunks - 1 - c)
        work = []
        for d in range(2):
            rows = pl.ds(pl.multiple_of(chunk[d] * c_len, c_len), c_len)
            dec = dec_ref[d, chunk[d]]
            for h in range(HG_HEADS):
                sl = slice(h * HG_DK, (h + 1) * HG_DK)
                qd = qd_ref[d, rows, sl]
                v = v_refs[d][rows, sl]
                st = s_refs[d][h]
                scores = lax.dot_general(qd, ki_ref[d, rows, sl], _NT, preferred_element_type=F32)
                inter = lax.dot_general(qd, st.astype(BF16), _NT, preferred_element_type=F32)
                inc = lax.dot_general(v, ke_ref[d, rows, sl], _TN, preferred_element_type=F32)
                s_refs[d][h] = st * dec[:, sl] + inc
                work.append((d, rows, sl, scores, inter, v))
        for d, rows, sl, scores, inter, v in work:
            intra = jnp.dot(jnp.where(masks[d], scores, 0.0).astype(BF16), v, preferred_element_type=F32)
            o_refs[d][rows, sl] = intra + inter
        return carry

    lax.fori_loop(0, nchunks, body, 0, unroll=math.gcd(nchunks, 4))


def _scan(p, lf, v_meta, lf_meta, batch, seq, rb):
    rows = batch * seq
    assert seq % rb == 0 and rb % HG_CHUNK == 0
    nb = seq // rb

    def fwd(col):
        return lambda b, n: (b * nb + n, col)

    def bwd(col):
        return lambda b, n: (b * nb + nb - 1 - n, col)

    return pl.pallas_call(
        functools.partial(_scan_kernel, nchunks=rb // HG_CHUNK),
        grid=(batch, nb),
        in_specs=[
            pl.BlockSpec((rb, HG_KDIM), fwd(PB_Q)),
            pl.BlockSpec((rb, HG_WIDTH), fwd(PB_I)),
            pl.BlockSpec((rb, HG_KDIM), fwd(0)),
            pl.BlockSpec((rb, HG_KDIM), bwd(PB_Q)),
            pl.BlockSpec((rb, HG_WIDTH), bwd(PB_I)),
            pl.BlockSpec((rb, HG_KDIM), bwd(1)),
            pl.BlockSpec((N_META, HG_WIDTH), lambda b, n: (0, 0)),
            pl.BlockSpec((N_META, HG_KDIM), lambda b, n: (0, 0)),
        ],
        out_specs=[
            pl.BlockSpec((rb, HG_WIDTH), fwd(0)),
            pl.BlockSpec((rb, HG_WIDTH), bwd(0)),
        ],
        out_shape=[
            jax.ShapeDtypeStruct((rows, HG_WIDTH), F32),
            jax.ShapeDtypeStruct((rows, HG_WIDTH), F32),
        ],
        scratch_shapes=[
            pltpu.VMEM((HG_HEADS, HG_DV, HG_DK), F32),
            pltpu.VMEM((HG_HEADS, HG_DV, HG_DK), F32),
            pltpu.VMEM((2, rb, HG_KDIM), BF16),
            pltpu.VMEM((2, rb, HG_KDIM), BF16),
            pltpu.VMEM((2, rb, HG_KDIM), BF16),
            pltpu.VMEM((2, rb // HG_CHUNK, 1, HG_KDIM), F32),
        ],
        compiler_params=pltpu.CompilerParams(
            dimension_semantics=("arbitrary", "arbitrary"), vmem_limit_bytes=VMEM_LIMIT_BYTES),
        name="hgrn2_scan",
    )(p, p, lf, p, p, lf, v_meta, lf_meta)


N_KEYS = 3 * ATT_BLOCK + N_META


def _alibi_slope(head):
    return 2.0 ** (-8.0 * (head + 1) / N_Q_HEADS)


LANES = 128
HEADS_PER_GROUP = LANES // HEAD_DIM


def _attn_kernel(sink_ref, q_ref, kvc_ref, kvp_ref, kvn_ref, kvm_ref, o_ref, bias_ref, *, seq, qb):
    n = pl.program_id(1)
    nsub = qb // ATT_BLOCK
    neg_inf = -jnp.inf

    @pl.when((pl.program_id(0) == 0) & (n == 0))
    def _():
        u = lax.broadcasted_iota(jnp.int32, (N_KEYS, ATT_BLOCK), 0)
        r = lax.broadcasted_iota(jnp.int32, (N_KEYS, ATT_BLOCK), 1)
        dist = jnp.abs(u - ATT_BLOCK - r)
        is_meta = u >= 3 * ATT_BLOCK
        dist_f = dist.astype(F32)
        for head in range(N_Q_HEADS):
            inside = jnp.where(dist <= WINDOW, (-_alibi_slope(head) * LOG2E) * dist_f, neg_inf)
            bias_ref[head] = jnp.where(is_meta, 0.0, inside)

    lane = lax.broadcasted_iota(jnp.int32, (N_KEYS, LANES), 1)
    low_half = lane < HEAD_DIM
    for i in range(nsub):
        rows = slice(i * ATT_BLOCK, (i + 1) * ATT_BLOCK)
        kv_prev = kvp_ref[...] if i == 0 else kvc_ref[(i - 1) * ATT_BLOCK:i * ATT_BLOCK, :]
        kv_next = kvn_ref[...] if i == nsub - 1 else kvc_ref[(i + 1) * ATT_BLOCK:(i + 2) * ATT_BLOCK, :]
        kv_all = jnp.concatenate([kv_prev, kvc_ref[rows, :], kv_next, kvm_ref[...]], axis=0)
        pen_prev = jnp.where(n * qb + (i - 1) * ATT_BLOCK < 0, neg_inf, 0.0) if i == 0 else None
        pen_next = jnp.where(n * qb + (i + 1) * ATT_BLOCK >= seq, neg_inf, 0.0) if i == nsub - 1 else None
        for m in range(N_KV_HEADS // HEADS_PER_GROUP):
            kcol = kv_all[:, m * LANES:(m + 1) * LANES]
            vcol = kv_all[:, KV_WIDTH + m * LANES:KV_WIDTH + (m + 1) * LANES]
            kswap = jnp.concatenate([kcol[:, HEAD_DIM:], kcol[:, :HEAD_DIM]], axis=1)
            probs, inv_denoms = [], []
            for e in range(HEADS_PER_GROUP):
                h = m * HEADS_PER_GROUP + e
                k_lo = jnp.where(low_half, kcol if e == 0 else kswap, 0).astype(BF16)
                k_hi = jnp.where(low_half, 0, kswap if e == 0 else kcol).astype(BF16)
                qcols = [q_ref[rows, (2 * h + c) * LANES:(2 * h + c + 1) * LANES] for c in range(2)]
                s_all = lax.dot_general(jnp.concatenate([k_lo, k_hi], axis=0), jnp.concatenate(qcols, axis=0),
                                        _NT, preferred_element_type=F32)
                for g in range(GROUP):
                    head = h * GROUP + g
                    half, c = g % 2, g // 2
                    s = s_all[half * N_KEYS:(half + 1) * N_KEYS, c * ATT_BLOCK:(c + 1) * ATT_BLOCK]
                    bias = bias_ref[head]
                    if pen_prev is not None or pen_next is not None:
                        pieces = [bias[0:ATT_BLOCK], bias[ATT_BLOCK:2 * ATT_BLOCK],
                                  bias[2 * ATT_BLOCK:3 * ATT_BLOCK], bias[3 * ATT_BLOCK:]]
                        if pen_prev is not None:
                            pieces[0] = pieces[0] + pen_prev
                        if pen_next is not None:
                            pieces[2] = pieces[2] + pen_next
                        bias = jnp.concatenate(pieces, axis=0)
                    s = s + bias
                    sink = sink_ref[head] * LOG2E
                    mx = jnp.maximum(jnp.max(s, axis=0, keepdims=True), sink)
                    p = jnp.exp2(s - mx)
                    denom = jnp.sum(p, axis=0, keepdims=True) + jnp.exp2(sink - mx)
                    probs.append(p.astype(BF16))
                    inv_denoms.append(1.0 / denom)
            o_t = lax.dot_general(vcol, jnp.concatenate(probs, axis=1), _TN, preferred_element_type=F32)
            for e in range(HEADS_PER_GROUP):
                h = m * HEADS_PER_GROUP + e
                for c in range(2):
                    pair = []
                    for half in range(2):
                        idx = e * GROUP + 2 * c + half
                        pair.append(o_t[e * HEAD_DIM:(e + 1) * HEAD_DIM, idx * ATT_BLOCK:(idx + 1) * ATT_BLOCK]
                                    * inv_denoms[idx])
                    out = jnp.concatenate(pair, axis=0).T
                    o_ref[rows, (2 * h + c) * LANES:(2 * h + c + 1) * LANES] = out.astype(BF16)


def _attn(p, kv_meta, sink, batch, seq, qb):
    rows = batch * seq
    assert seq % qb == 0 and qb % ATT_BLOCK == 0
    nb = seq // qb
    per = qb // ATT_BLOCK
    last = rows // ATT_BLOCK - 1
    q_col, kv_col = PB_AQ, PB_KV

    return pl.pallas_call(
        functools.partial(_attn_kernel, seq=seq, qb=qb),
        grid=(batch, nb),
        in_specs=[
            pl.BlockSpec(memory_space=pltpu.SMEM),
            pl.BlockSpec((qb, ATT_WIDTH), lambda b, n: (b * nb + n, q_col)),
            pl.BlockSpec((qb, 2 * KV_WIDTH), lambda b, n: (b * nb + n, kv_col)),
            pl.BlockSpec((ATT_BLOCK, 2 * KV_WIDTH),
                         lambda b, n: (jnp.maximum((b * nb + n) * per - 1, 0), kv_col)),
            pl.BlockSpec((ATT_BLOCK, 2 * KV_WIDTH),
                         lambda b, n: (jnp.minimum((b * nb + n + 1) * per, last), kv_col)),
            pl.BlockSpec((N_META, 2 * KV_WIDTH), lambda b, n: (0, 0)),
        ],
        out_specs=pl.BlockSpec((qb, ATT_WIDTH), lambda b, n: (b * nb + n, 0)),
        out_shape=jax.ShapeDtypeStruct((rows, ATT_WIDTH), BF16),
        scratch_shapes=[pltpu.VMEM((N_Q_HEADS, N_KEYS, ATT_BLOCK), F32)],
        compiler_params=pltpu.CompilerParams(
            dimension_semantics=("arbitrary", "arbitrary"), vmem_limit_bytes=VMEM_LIMIT_BYTES),
        name="window_attn",
    )(sink, p, p, p, p, kv_meta)


def _merge_kernel(of_ref, ob_ref, g_ref, att_ref, gt0_ref, gt1_ref, gt2_ref, gt3_ref, gt4_ref, x_ref,
                  wa_ref, wb_ref, wo_ref, hgain_ref, gpost_ref, gpre_ref, x1_ref, h2_ref, *, nsplit):
    sub = x_ref.shape[0] // nsplit
    half = P_GATE_START % IN_TILE
    for r in range(nsplit):
        rows = slice(r * sub, (r + 1) * sub)
        gate_a = jnp.concatenate([gt0_ref[rows, half:], gt1_ref[rows, :], gt2_ref[rows, :half]], axis=1)
        gate_b = jnp.concatenate([gt2_ref[rows, half:], gt3_ref[rows, :], gt4_ref[rows, :half]], axis=1)
        y = gate_b.astype(F32) * jnp.dot(att_ref[rows, :], wb_ref[...], preferred_element_type=F32)
        o = of_ref[rows, :] + ob_ref[rows, :]
        parts = []
        for h in range(HG_HEADS):
            oh = o[:, h * HG_DV:(h + 1) * HG_DV]
            parts.append(oh * _rms_scale(oh))
        a = jnp.concatenate(parts, axis=1) * hgain_ref[...] * g_ref[rows, :].astype(F32)
        y = y + gate_a.astype(F32) * jnp.dot(a.astype(BF16), wa_ref[...], preferred_element_type=F32)
        z = jnp.dot(y.astype(BF16), wo_ref[...], preferred_element_type=F32)
        x1 = x_ref[rows, :] + z * _rms_scale(z) * gpost_ref[...]
        x1_ref[rows, :] = x1
        h2_ref[rows, :] = (x1 * _rms_scale(x1) * gpre_ref[...]).astype(BF16)


def _merge(o_f, o_b, p, att, x, wa, wb, wo, hgain, gpost, gpre, tm):
    rows = x.shape[0]
    assert rows % tm == 0
    row = lambda c: (lambda i: (i, c))
    const = lambda i: (0, 0)
    single = pl.Buffered(1)
    nsplit = 2 if tm % 256 == 0 else 1
    return pl.pallas_call(
        functools.partial(_merge_kernel, nsplit=nsplit),
        grid=(rows // tm,),
        in_specs=[
            pl.BlockSpec((tm, HG_WIDTH), row(0)),
            pl.BlockSpec((tm, HG_WIDTH), row(0)),
            pl.BlockSpec((tm, HG_WIDTH), row(PB_G)),
            pl.BlockSpec((tm, ATT_WIDTH), row(0)),
            *[pl.BlockSpec((tm, IN_TILE), row(t)) for t in PB_GATE_TILES],
            pl.BlockSpec((tm, D_MODEL), row(0)),
            pl.BlockSpec((HG_WIDTH, D_MODEL), const, pipeline_mode=single),
            pl.BlockSpec((ATT_WIDTH, D_MODEL), const, pipeline_mode=single),
            pl.BlockSpec((D_MODEL, D_MODEL), const, pipeline_mode=single),
            pl.BlockSpec((1, HG_WIDTH), const),
            pl.BlockSpec((1, D_MODEL), const),
            pl.BlockSpec((1, D_MODEL), const),
        ],
        out_specs=[
            pl.BlockSpec((tm, D_MODEL), row(0)),
            pl.BlockSpec((tm, D_MODEL), row(0)),
        ],
        out_shape=[
            jax.ShapeDtypeStruct((rows, D_MODEL), F32),
            jax.ShapeDtypeStruct((rows, D_MODEL), BF16),
        ],
        compiler_params=pltpu.CompilerParams(
            dimension_semantics=("arbitrary",), vmem_limit_bytes=VMEM_LIMIT_BYTES),
        name="merge_out",
    )(o_f, o_b, p, att, *([p] * len(PB_GATE_TILES)), x, wa, wb, wo, hgain, gpost, gpre)


FF_UP_TILE = 1024
FF_DOWN_TILE = 512


def _ffn_up_kernel(h_ref, w1_ref, u_ref, *, nsplit):
    sub = h_ref.shape[0] // nsplit
    w1 = w1_ref[...].astype(BF16)
    for r in range(nsplit):
        rows = slice(r * sub, (r + 1) * sub)
        u = jnp.maximum(jnp.dot(h_ref[rows, :], w1, preferred_element_type=F32), 0.0)
        u_ref[rows, :] = (u * u).astype(BF16)


def _ffn_up(h2, w1, tm):
    rows = h2.shape[0]
    assert rows % tm == 0
    nsplit = 4 if tm % (4 * 128) == 0 else 1
    n_tiles = D_FF // FF_UP_TILE
    return pl.pallas_call(
        functools.partial(_ffn_up_kernel, nsplit=nsplit),
        grid=(rows // tm, n_tiles),
        in_specs=[
            pl.BlockSpec((tm, D_MODEL), lambda i, j: (i, 0)),
            pl.BlockSpec((D_MODEL, FF_UP_TILE), lambda i, j: (0, _serpentine(i, j, n_tiles))),
        ],
        out_specs=pl.BlockSpec((tm, FF_UP_TILE), lambda i, j: (i, _serpentine(i, j, n_tiles))),
        out_shape=jax.ShapeDtypeStruct((rows, D_FF), BF16),
        compiler_params=pltpu.CompilerParams(
            dimension_semantics=("arbitrary", "arbitrary"), vmem_limit_bytes=VMEM_LIMIT_BYTES),
        name="ffn_up",
    )(h2, w1)


def _ffn_down_kernel(u_ref, x_ref, w2_ref, g_ref, y_ref):
    tile = _serpentine(pl.program_id(0), pl.program_id(1), pl.num_programs(1))
    cols = pl.ds(pl.multiple_of(tile * FF_DOWN_TILE, FF_DOWN_TILE), FF_DOWN_TILE)
    last = pl.program_id(1) == pl.num_programs(1) - 1

    @pl.when(jnp.logical_not(last))
    def _():
        y_ref[:, cols] = jnp.dot(u_ref[...], w2_ref[...], preferred_element_type=F32)

    @pl.when(last)
    def _():
        half = y_ref.shape[0] // 2
        for r in range(2):
            rows = slice(r * half, (r + 1) * half)
            y_ref[rows, cols] = jnp.dot(u_ref[rows, :], w2_ref[...], preferred_element_type=F32)
            ff = y_ref[rows, :]
            y_ref[rows, :] = x_ref[rows, :] + ff * _rms_scale(ff) * g_ref[...]


def _ffn_down(u, x1, w2, gpost, tm):
    rows = x1.shape[0]
    assert rows % tm == 0
    return pl.pallas_call(
        _ffn_down_kernel,
        grid=(rows // tm, D_MODEL // FF_DOWN_TILE),
        in_specs=[
            pl.BlockSpec((tm, D_FF), lambda i, j: (i, 0)),
            pl.BlockSpec((tm, D_MODEL), lambda i, j: (i, 0)),
            pl.BlockSpec((D_FF, FF_DOWN_TILE), lambda i, j: (0, _serpentine(i, j, D_MODEL // FF_DOWN_TILE))),
            pl.BlockSpec((1, D_MODEL), lambda i, j: (0, 0)),
        ],
        out_specs=pl.BlockSpec((tm, D_MODEL), lambda i, j: (i, 0)),
        out_shape=jax.ShapeDtypeStruct((rows, D_MODEL), F32),
        compiler_params=pltpu.CompilerParams(
            dimension_semantics=("arbitrary", "arbitrary"), vmem_limit_bytes=VMEM_LIMIT_BYTES),
        name="ffn_down",
    )(u, x1, w2, gpost)


def _tile(n, pref):
    return pref if n % pref == 0 else math.gcd(n, pref)


def _trunk(x, meta_proj, w, batch, seq):
    rows = batch * seq
    xf = x.reshape(rows, D_MODEL)
    p, lf = _in_proj(xf, w["g_pre_mix"], w["w_in"], w["lb_logits"], _tile(rows, 1024))
    v_m, kv_m, lf_m = meta_proj
    o_f, o_b = _scan(p, lf, v_m, lf_m, batch, seq, _tile(seq, 512))
    att = _attn(p, kv_m, w["attn_sink"], batch, seq, _tile(seq, 2048))
    x1, h2 = _merge(o_f, o_b, p, att, xf, w["w_proj_hg"], w["w_proj_att"], w["w_out"],
                    w["hg_out_gain"], w["g_post_mix"], w["g_pre_ff"], _tile(rows, 256))
    u = _ffn_up(h2, w["w_ff1"], _tile(rows, 2048))
    y = _ffn_down(u, x1, w["w_ff2"], w["g_post_ff"], _tile(rows, 512))
    return y.reshape(batch, seq, D_MODEL)


def _prepare(meta_tokens, w_in, w_proj_hg, w_proj_att, w_out, w_ff1, w_ff2, g_pre_mix, g_post_mix,
             g_pre_ff, g_post_ff, lb_logits, hg_out_gain, attn_sink):
    w = {
        "w_in": w_in[0].astype(F32),
        "w_proj_hg": w_proj_hg[0].astype(BF16),
        "w_proj_att": w_proj_att[0].astype(BF16),
        "w_out": w_out[0].astype(BF16),
        "w_ff1": w_ff1[0].astype(F32),
        "w_ff2": w_ff2[0].astype(BF16),
        "g_pre_mix": g_pre_mix[0].reshape(1, D_MODEL).astype(F32),
        "g_post_mix": g_post_mix[0].reshape(1, D_MODEL).astype(F32),
        "g_pre_ff": g_pre_ff[0].reshape(1, D_MODEL).astype(F32),
        "g_post_ff": g_post_ff[0].reshape(1, D_MODEL).astype(F32),
        "lb_logits": lb_logits.reshape(2, 2 * HG_KDIM).astype(F32),
        "hg_out_gain": hg_out_gain[0].reshape(1, HG_WIDTH).astype(F32),
        "attn_sink": attn_sink[0].astype(F32),
    }
    meta_proj = _meta_proj(meta_tokens.astype(F32), w["g_pre_mix"], w["w_in"], w["lb_logits"])
    return w, meta_proj


def kernel(x_prompt, x_sample, meta_tokens, w_in, w_proj_hg, w_proj_att, w_out, w_ff1, w_ff2, g_pre_mix, g_post_mix, g_pre_ff, g_post_ff, lb_logits, hg_out_gain, attn_sink):
    w, meta_proj = _prepare(meta_tokens, w_in, w_proj_hg, w_proj_att, w_out, w_ff1, w_ff2, g_pre_mix,
                            g_post_mix, g_pre_ff, g_post_ff, lb_logits, hg_out_gain, attn_sink)
    y_prompt = _trunk(x_prompt, meta_proj, w, x_prompt.shape[0], x_prompt.shape[1])
    y_sample = _trunk(x_sample, meta_proj, w, x_sample.shape[0], x_sample.shape[1])
    return (y_prompt, y_sample)
```
